```python
import math
import jax, jax.numpy as jnp
from jax import lax
import numpy as np

D_MODEL = 2048
BATCH = 4
SEQ = 2048
DEPTH = 2
DEC_BATCH = 128
DEC_SEQ = 4
PAST_LEN = 2048
PAGE_SIZE = 128

HEAD_DIM = 128
H_A = 8
KV_A = 2
G_A = H_A // KV_A
W_A = H_A * HEAD_DIM
H_I = 16
D_IDX = 64
TOPK_MAX = 256
H_B = 8
W_B = H_B * HEAD_DIM
N_BUCKETS = 32
MAX_DISTANCE = 128
PLE_DIM = 256
Q_BLOCK = 128
LN_EPS = 1e-5
ALPHA = (2 * DEPTH) ** 0.25
BETA_INIT = (8 * DEPTH) ** -0.25
SPLITS = (W_A, KV_A * HEAD_DIM, KV_A * HEAD_DIM, W_A,
          H_I * D_IDX, D_IDX, H_I,
          W_B, W_B, W_B, W_B,
          D_MODEL, D_MODEL)
D_IN = sum(SPLITS)
SPLIT_POINTS = tuple(sum(SPLITS[:i + 1]) for i in range(len(SPLITS) - 1))

kernel_name = 'dsa_stickbreaking_gated_hybrid_step'


def layer_norm(x, g, b):
    xf = x.astype(jnp.float32)
    mu = jnp.mean(xf, axis=-1, keepdims=True)
    var = jnp.mean(jnp.square(xf - mu), axis=-1, keepdims=True)
    return ((xf - mu) * lax.rsqrt(var + LN_EPS) * g + b).astype(x.dtype)


def t5_bucket(dist):
    n = jnp.maximum(dist, 0)
    max_exact = N_BUCKETS // 2
    nf = jnp.maximum(n, 1).astype(jnp.float32)
    large = max_exact + (jnp.log(nf / max_exact) / math.log(MAX_DISTANCE / max_exact)
                         * (N_BUCKETS - max_exact)).astype(jnp.int32)
    return jnp.where(n < max_exact, n, jnp.minimum(large, N_BUCKETS - 1))


def take_rows(a, idx):
    return jax.vmap(lambda ab, ib: ab[ib])(a, idx)


def branch_inputs(x, w_in, kidx_g, kidx_b):
    B, T, _ = x.shape
    h = jnp.einsum('btd,dc->btc', x, w_in)
    qa, ka, va, ga, qi, ki, wi, qb, kb, vb, gb, ma, mb = jnp.split(h, SPLIT_POINTS, axis=-1)
    heads = (qa.reshape(B, T, KV_A, G_A, HEAD_DIM),
             ka.reshape(B, T, KV_A, HEAD_DIM),
             va.reshape(B, T, KV_A, HEAD_DIM),
             qi.reshape(B, T, H_I, D_IDX),
             layer_norm(ki, kidx_g, kidx_b),
             wi * H_I ** -0.5,
             qb.reshape(B, T, H_B, HEAD_DIM),
             kb.reshape(B, T, H_B, HEAD_DIM),
             vb.reshape(B, T, H_B, HEAD_DIM))
    gates = (ga, gb, ma, mb)
    return heads, gates


def dsa_block(q, qi, wi, q_pos, kidx_all, gather_kv, rel_bias, topk):
    B, T = q.shape[0], q.shape[1]
    L = kidx_all.shape[1]
    dots = jnp.einsum('bthi,bsi->bths', qi, kidx_all).astype(jnp.float32) * D_IDX ** -0.5
    score = jnp.einsum('bth,bths->bts', wi.astype(jnp.float32), jax.nn.relu(dots))
    admissible = jnp.arange(L, dtype=jnp.int32)[None, :] <= q_pos[:, None]
    score = jnp.where(admissible[None], score, -jnp.inf)
    _, idx = lax.top_k(score, topk)
    k_sel, v_sel = gather_kv(idx)
    logits = jnp.einsum('btgnd,btkgd->btgnk', q, k_sel).astype(jnp.float32) * HEAD_DIM ** -0.5
    bias = rel_bias[t5_bucket(q_pos[None, :, None] - idx)].astype(jnp.float32)
    bias = jnp.transpose(bias.reshape(B, T, topk, KV_A, G_A), (0, 1, 3, 4, 2))
    valid = (idx <= q_pos[None, :, None])[:, :, None, None, :]
    p = jax.nn.softmax(jnp.where(valid, logits + bias, -jnp.inf), axis=-1)
    o = jnp.einsum('btgnk,btkgd->btgnd', p.astype(v_sel.dtype), v_sel)
    return o.reshape(B, T, W_A)


def stick_breaking_block(q, k, v, q_pos):
    B, T = q.shape[0], q.shape[1]
    L = k.shape[1]
    z = jnp.einsum('bthd,bshd->bhts', q, k).astype(jnp.float32) * HEAD_DIM ** -0.5
    visible = (jnp.arange(L, dtype=jnp.int32)[None, :] < q_pos[:, None])[None, None]
    log_1m = jnp.where(visible, jax.nn.log_sigmoid(-z), 0.0)
    suffix = lax.cumsum(log_1m, axis=3, reverse=True) - log_1m
    a = jnp.where(visible, jnp.exp(jax.nn.log_sigmoid(z) + suffix), 0.0)
    o = jnp.einsum('bhts,bshd->bthd', a.astype(v.dtype), v)
    return o.reshape(B, T, W_B)


def prompt_mix(heads, rel_bias):
    qa, ka, va, qi, ki, wi, qb, kb, vb = heads
    B = qa.shape[0]
    topk = min(TOPK_MAX, SEQ // 4)

    def gather(idx):
        return take_rows(ka, idx), take_rows(va, idx)

    def blk(j):
        start = j * Q_BLOCK
        q_pos = start + jnp.arange(Q_BLOCK, dtype=jnp.int32)
        sl = lambda a: lax.dynamic_slice_in_dim(a, start, Q_BLOCK, axis=1)
        o_a = dsa_block(sl(qa), sl(qi), sl(wi), q_pos, ki, gather, rel_bias, topk)
        o_b = stick_breaking_block(sl(qb), kb, vb, q_pos)
        return o_a, o_b

    o_a, o_b = lax.map(blk, jnp.arange(SEQ // Q_BLOCK, dtype=jnp.int32))
    unblock = lambda o: jnp.moveaxis(o, 0, 1).reshape(B, SEQ, o.shape[-1])
    return unblock(o_a), unblock(o_b)


def sample_mix(heads, cache_k_a, cache_v_a, cache_kidx, cache_k_b, cache_v_b, page_table, layer, rel_bias):
    qa, ka, va, qi, ki, wi, qb, kb, vb = heads
    DB = qa.shape[0]
    L = PAST_LEN + DEC_SEQ
    q_pos = PAST_LEN + jnp.arange(DEC_SEQ, dtype=jnp.int32)

    def past(c):
        return c[page_table, layer].reshape((DB, PAST_LEN) + c.shape[3:])

    ki_all = jnp.concatenate([past(cache_kidx), ki], axis=1)

    def gather(idx):
        in_past = (idx < PAST_LEN)[..., None, None]
        pidx = jnp.minimum(idx, PAST_LEN - 1)
        phys = jax.vmap(lambda pt, i: pt[i])(page_table, pidx // PAGE_SIZE)
        off = pidx % PAGE_SIZE
        nidx = jnp.clip(idx - PAST_LEN, 0, DEC_SEQ - 1)
        k_sel = jnp.where(in_past, cache_k_a[phys, layer, off], take_rows(ka, nidx))
        v_sel = jnp.where(in_past, cache_v_a[phys, layer, off], take_rows(va, nidx))
        return k_sel, v_sel

    o_a = dsa_block(qa, qi, wi, q_pos, ki_all, gather, rel_bias, min(TOPK_MAX, L // 4))
    kb_all = jnp.concatenate([past(cache_k_b), kb], axis=1)
    vb_all = jnp.concatenate([past(cache_v_b), vb], axis=1)
    o_b = stick_breaking_block(qb, kb_all, vb_all, q_pos)
    return o_a, o_b


def layer_update(x, gates, o_a, o_b, p, w_pa, w_pb, w_out, ln1_g, ln1_b, w_pe, w_pg, ln2_g, ln2_b):
    ga, gb, ma, mb = gates
    y_a = jnp.einsum('btc,cd->btd', o_a * jax.nn.silu(ga), w_pa)
    y_b = jnp.einsum('btc,cd->btd', o_b * jax.nn.silu(gb), w_pb)
    mix = jnp.einsum('btd,de->bte', jax.nn.sigmoid(ma) * y_a + jax.nn.sigmoid(mb) * y_b, w_out)
    x = layer_norm(ALPHA * x + mix, ln1_g, ln1_b)
    ple = jax.nn.sigmoid(jnp.einsum('btd,de->bte', x, w_pg)) * jnp.einsum('btp,pd->btd', p, w_pe)
    return layer_norm(ALPHA * x + ple, ln2_g, ln2_b)


def setup_inputs(seed: int = 0) -> dict:
    key = jax.random.key(seed)
    ks = jax.random.split(key, 24)
    f32 = jnp.float32
    nrm = lambda k, shape, scale=1.0: jax.random.normal(k, shape, f32) * scale
    n_pages = PAST_LEN // PAGE_SIZE
    n_used = DEC_BATCH * n_pages
    n_pool = n_used + max(1, n_used // 4)
    page_table = jax.random.permutation(ks[0], n_pool)[:n_used].reshape(DEC_BATCH, n_pages).astype(jnp.int32)
    return {
        'x_prompt': nrm(ks[1], (BATCH, SEQ, D_MODEL)),
        'x_sample': nrm(ks[2], (DEC_BATCH, DEC_SEQ, D_MODEL)),
        'cache_k_a': nrm(ks[3], (n_pool, DEPTH, PAGE_SIZE, KV_A, HEAD_DIM)),
        'cache_v_a': nrm(ks[4], (n_pool, DEPTH, PAGE_SIZE, KV_A, HEAD_DIM)),
        'cache_kidx': nrm(ks[5], (n_pool, DEPTH, PAGE_SIZE, D_IDX)),
        'cache_k_b': nrm(ks[6], (n_pool, DEPTH, PAGE_SIZE, H_B, HEAD_DIM)),
        'cache_v_b': nrm(ks[7], (n_pool, DEPTH, PAGE_SIZE, H_B, HEAD_DIM)),
        'page_table': page_table,
        'p_prompt': nrm(ks[8], (DEPTH, BATCH, SEQ, PLE_DIM)),
        'p_sample': nrm(ks[9], (DEPTH, DEC_BATCH, DEC_SEQ, PLE_DIM)),
        'w_in': nrm(ks[10], (DEPTH, D_MODEL, D_IN), D_MODEL ** -0.5),
        'ln_kidx_g': 1.0 + nrm(ks[11], (DEPTH, D_IDX), 0.02),
        'ln_kidx_b': nrm(ks[12], (DEPTH, D_IDX), 0.02),
        'rel_bias': nrm(ks[13], (N_BUCKETS, H_A), 0.1),
        'w_pa': nrm(ks[14], (DEPTH, W_A, D_MODEL), BETA_INIT * W_A ** -0.5),
        'w_pb': nrm(ks[15], (DEPTH, W_B, D_MODEL), BETA_INIT * W_B ** -0.5),
        'w_out': nrm(ks[16], (DEPTH, D_MODEL, D_MODEL), BETA_INIT * D_MODEL ** -0.5),
        'ln1_g': 1.0 + nrm(ks[17], (DEPTH, D_MODEL), 0.02),
        'ln1_b': nrm(ks[18], (DEPTH, D_MODEL), 0.02),
        'w_pe': nrm(ks[19], (DEPTH, PLE_DIM, D_MODEL), BETA_INIT * PLE_DIM ** -0.5),
        'w_pg': nrm(ks[20], (DEPTH, D_MODEL, D_MODEL), D_MODEL ** -0.5),
        'ln2_g': 1.0 + nrm(ks[21], (DEPTH, D_MODEL), 0.02),
        'ln2_b': nrm(ks[22], (DEPTH, D_MODEL), 0.02),
    }


def reference(x_prompt, x_sample, cache_k_a, cache_v_a, cache_kidx, cache_k_b, cache_v_b, page_table,
              p_prompt, p_sample, w_in, ln_kidx_g, ln_kidx_b, rel_bias, w_pa, w_pb, w_out,
              ln1_g, ln1_b, w_pe, w_pg, ln2_g, ln2_b):
    xp, xs = x_prompt, x_sample
    names = ('k_a', 'v_a', 'kidx', 'k_b', 'v_b')
    rows_p = {n: [] for n in names}
    rows_s = {n: [] for n in names}
    for i in range(DEPTH):
        hp, gp = branch_inputs(xp, w_in[i], ln_kidx_g[i], ln_kidx_b[i])
        hs, gs = branch_inputs(xs, w_in[i], ln_kidx_g[i], ln_kidx_b[i])
        for n, r in zip(names, (hp[1], hp[2], hp[4], hp[7], hp[8])):
            rows_p[n].append(r)
        for n, r in zip(names, (hs[1], hs[2], hs[4], hs[7], hs[8])):
            rows_s[n].append(r)
        oa_p, ob_p = prompt_mix(hp, rel_bias)
        oa_s, ob_s = sample_mix(hs, cache_k_a, cache_v_a, cache_kidx, cache_k_b, cache_v_b,
                                page_table, i, rel_bias)
        xp = layer_update(xp, gp, oa_p, ob_p, p_prompt[i], w_pa[i], w_pb[i], w_out[i],
                          ln1_g[i], ln1_b[i], w_pe[i], w_pg[i], ln2_g[i], ln2_b[i])
        xs = layer_update(xs, gs, oa_s, ob_s, p_sample[i], w_pa[i], w_pb[i], w_out[i],
                          ln1_g[i], ln1_b[i], w_pe[i], w_pg[i], ln2_g[i], ln2_b[i])
    st = lambda rows: jnp.stack(rows, axis=1)
    return (xp, xs,
            st(rows_p['k_a']), st(rows_p['v_a']), st(rows_p['kidx']), st(rows_p['k_b']), st(rows_p['v_b']),
            st(rows_s['k_a']), st(rows_s['v_a']), st(rows_s['kidx']), st(rows_s['k_b']), st(rows_s['v_b']))
```

```python
import functools
import math

import jax
import jax.numpy as jnp
from jax import lax
from jax.experimental import pallas as pl
from jax.experimental.pallas import tpu as pltpu

F32 = jnp.float32
BF16 = jnp.bfloat16
I32 = jnp.int32

D_MODEL = 2048
BATCH = 4
SEQ = 2048
DEPTH = 2
DEC_BATCH = 128
DEC_SEQ = 4
PAST_LEN = 2048
PAGE_SIZE = 128
N_PAGES = PAST_LEN // PAGE_SIZE
HEAD_DIM = 128
H_A = 8
KV_A = 2
G_A = H_A // KV_A
W_A = H_A * HEAD_DIM
H_I = 16
D_IDX = 64
TOPK = 256
H_B = 8
W_B = H_B * HEAD_DIM
N_BUCKETS = 32
MAX_DISTANCE = 128
PLE_DIM = 256
LN_EPS = 1e-5
ALPHA = (2 * DEPTH) ** 0.25

M_PROMPT = BATCH * SEQ
M_SAMPLE = DEC_BATCH * DEC_SEQ
M_ALL = M_PROMPT + M_SAMPLE

LANES = 128
SUBLANES = 8
Q_BLK = 128
CK = 256
N_QBLK = SEQ // Q_BLK
NEG = -1e30
INT_MIN = -2 ** 31
ATT_SCALE = HEAD_DIM ** -0.5
VMEM_LIMIT = 52 * 1024 * 1024

SEQ_GROUP = 32
N_SEQ_GROUPS = DEC_BATCH // SEQ_GROUP
S_TILES = N_PAGES + 1
PAGES_PER_STEP = 4
N_PAGE_STEPS = N_PAGES // PAGES_PER_STEP
NEW_B = LANES // H_B
NEW_A = LANES // KV_A


def _cparams(n_axes):
    return pltpu.CompilerParams(dimension_semantics=("arbitrary",) * n_axes,
                                vmem_limit_bytes=VMEM_LIMIT)


def _dot_nt(a, b):
    return lax.dot_general(a, b, (((1,), (1,)), ((), ())), preferred_element_type=F32)


def _dot(a, b):
    return jnp.dot(a, b, preferred_element_type=F32)


def _sortable(s):
    b = pltpu.bitcast(s, I32)
    return jnp.where(b < 0, b ^ jnp.int32(0x7FFFFFFF), b)


def _split_dot(x, u):
    hi = x.astype(BF16)
    lo = (x - hi.astype(F32)).astype(BF16)
    return _dot(hi, u) + _dot(lo, u)


def _softplus(z):
    return jnp.maximum(z, 0.0) + jnp.log1p(jnp.exp(-jnp.abs(z)))


def _layer_norm_rows(x, g, b):
    mu = jnp.mean(x, axis=-1, keepdims=True)
    xc = x - mu
    var = jnp.mean(xc * xc, axis=-1, keepdims=True)
    return xc * lax.rsqrt(var + LN_EPS) * g + b


def _mm_kernel(x_ref, w_ref, *o_refs):
    acc = _dot(x_ref[...], w_ref[...])
    for o in o_refs:
        o[...] = acc.astype(o.dtype)


def _matmul(x, w, out_dtypes, tm=512, tn=512):
    m, k = x.shape
    n = w.shape[1]
    return pl.pallas_call(
        _mm_kernel,
        grid=(n // tn, m // tm),
        in_specs=[pl.BlockSpec((tm, k), lambda j, i: (i, 0)),
                  pl.BlockSpec((k, tn), lambda j, i: (0, j))],
        out_specs=[pl.BlockSpec((tm, tn), lambda j, i: (i, j)) for _ in out_dtypes],
        out_shape=[jax.ShapeDtypeStruct((m, n), d) for d in out_dtypes],
        compiler_params=_cparams(2),
        name="proj_matmul",
    )(x, w)


def _kiw_kernel(x_ref, w_ref, g_ref, b_ref, kf_ref, kb_ref, wi_ref):
    h = _dot(x_ref[...], w_ref[...])
    k2 = _layer_norm_rows(h[:, :LANES], g_ref[...], b_ref[...])
    kf_ref[...] = k2
    kb_ref[...] = k2.astype(BF16)
    wi_ref[...] = h[:, LANES:] * (H_I ** -0.5 * D_IDX ** -0.5)


def _kiw(x, w4, g2, b2, tm=512):
    m, k = x.shape
    return pl.pallas_call(
        _kiw_kernel,
        grid=(m // tm,),
        in_specs=[pl.BlockSpec((tm, k), lambda i: (i, 0)),
                  pl.BlockSpec((k, 2 * LANES), lambda i: (0, 0)),
                  pl.BlockSpec((1, LANES), lambda i: (0, 0)),
                  pl.BlockSpec((1, LANES), lambda i: (0, 0))],
        out_specs=[pl.BlockSpec((tm, LANES), lambda i: (i, 0))] * 3,
        out_shape=[jax.ShapeDtypeStruct((m, LANES), F32),
                   jax.ShapeDtypeStruct((m, LANES), BF16),
                   jax.ShapeDtypeStruct((m, LANES), F32)],
        compiler_params=_cparams(1),
        name="indexer_key_proj",
    )(x, w4, g2, b2)


def _t5_bucket(d):
    n = jnp.maximum(d, 0)
    max_exact = N_BUCKETS // 2
    nf = jnp.maximum(n, 1).astype(F32)
    large = max_exact + (jnp.log(nf / max_exact) / math.log(MAX_DISTANCE / max_exact)
                         * (N_BUCKETS - max_exact)).astype(I32)
    return jnp.where(n < max_exact, n, jnp.minimum(large, N_BUCKETS - 1))


def _bias_kernel(rb_ref, tiles_ref, strip_ref, far_ref):
    ii = lax.broadcasted_iota(I32, (Q_BLK, LANES), 0)
    jj = lax.broadcasted_iota(I32, (Q_BLK, LANES), 1)
    bk0 = _t5_bucket(ii - jj)
    bk1 = _t5_bucket(ii - jj + LANES)
    for n in range(H_A):
        t0 = jnp.zeros((Q_BLK, LANES), F32)
        t1 = jnp.zeros((Q_BLK, LANES), F32)
        for b in range(N_BUCKETS):
            t0 = jnp.where(bk0 == b, rb_ref[b, n], t0)
            t1 = jnp.where(bk1 == b, rb_ref[b, n], t1)
        tiles_ref[n, 0] = t0
        tiles_ref[n, 1] = t1
        tiles_ref[n, 2] = jnp.full((Q_BLK, LANES), rb_ref[N_BUCKETS - 1, n], F32)
    rows = lax.broadcasted_iota(I32, (DEC_SEQ * H_A, 2 * CK), 0)
    lan = lax.broadcasted_iota(I32, (DEC_SEQ * H_A, 2 * CK), 1)
    q = rows // H_A
    n_of_row = rows % H_A
    pos = jnp.where(lan < CK, PAST_LEN - PAGE_SIZE + lan // KV_A, PAST_LEN + (lan - CK) // KV_A)
    bks = _t5_bucket(PAST_LEN + q - pos)
    strip = jnp.zeros((DEC_SEQ * H_A, 2 * CK), F32)
    far = jnp.zeros((DEC_SEQ * H_A, LANES), F32)
    rows_f = lax.broadcasted_iota(I32, (DEC_SEQ * H_A, LANES), 0) % H_A
    for n in range(H_A):
        sn = jnp.zeros((DEC_SEQ * H_A, 2 * CK), F32)
        for b in range(N_BUCKETS):
            sn = jnp.where(bks == b, rb_ref[b, n], sn)
        strip = jnp.where(n_of_row == n, sn, strip)
        far = jnp.where(rows_f == n, rb_ref[N_BUCKETS - 1, n], far)
    strip_ref[...] = strip
    far_ref[...] = far


def _bias_tables(rel_bias):
    return pl.pallas_call(
        _bias_kernel,
        in_specs=[pl.BlockSpec(memory_space=pltpu.SMEM)],
        out_shape=[jax.ShapeDtypeStruct((H_A, 3, Q_BLK, LANES), F32),
                   jax.ShapeDtypeStruct((DEC_SEQ * H_A, 2 * CK), F32),
                   jax.ShapeDtypeStruct((DEC_SEQ * H_A, LANES), F32)],
        name="t5_bias_tables",
    )(rel_bias)


def _topk_threshold(count_ge, rows):
    c0 = count_ge(jnp.zeros((rows, 1), I32))
    t_init = jnp.where(c0 >= TOPK, jnp.int32(0), jnp.int32(INT_MIN))

    def bit_step(i, t):
        cand = t | jnp.left_shift(jnp.int32(1), 30 - i)
        return jnp.where(count_ge(cand) >= TOPK, cand, t)

    return lax.fori_loop(0, 31, bit_step, t_init)


def _dsa_prompt_kernel(qa_ref, qi_ref, wi_ref, ki2_ref, ka_ref, va_ref, bias_ref, utri_ref,
                       o_ref, qim_ref, key_ref, am_ref):
    j = pl.program_id(1)
    nck = j // 2 + 1
    t0 = j * Q_BLK

    lane = lax.broadcasted_iota(I32, (Q_BLK, LANES), 1)
    for pr in range(H_I // 2):
        qp = qi_ref[:, pr * LANES:(pr + 1) * LANES].astype(F32)
        qim_ref[2 * pr] = jnp.where(lane < D_IDX, qp, 0.0).astype(BF16)
        qim_ref[2 * pr + 1] = jnp.where(lane >= D_IDX, qp, 0.0).astype(BF16)

    rows = t0 + lax.broadcasted_iota(I32, (Q_BLK, CK), 0)
    cols0 = lax.broadcasted_iota(I32, (Q_BLK, CK), 1)
    wi = wi_ref[...]

    def score_chunk(c, carry):
        kc = ki2_ref[pl.ds(pl.multiple_of(c * CK, CK), CK), :]
        s = jnp.zeros((Q_BLK, CK), F32)
        for h in range(H_I):
            d = _dot_nt(qim_ref[h], kc)
            s = s + wi[:, h:h + 1] * jnp.maximum(d, 0.0)
        adm = (cols0 + c * CK) <= rows
        key_ref[c] = _sortable(jnp.where(adm, s, -jnp.inf))
        return carry

    lax.fori_loop(0, nck, score_chunk, 0)

    def count_ge(cand):
        cb = jnp.broadcast_to(cand, (Q_BLK, CK))

        def body(c, acc):
            return acc + jnp.where(key_ref[c] >= cb, 1.0, 0.0)

        acc = lax.fori_loop(0, nck, body, jnp.zeros((Q_BLK, CK), F32))
        return jnp.sum(acc, axis=1, keepdims=True)

    thr = _topk_threshold(count_ge, Q_BLK)
    tb = jnp.broadcast_to(thr, (Q_BLK, CK))

    def mask_chunk(c, acc):
        adm = (cols0 + c * CK) <= rows
        sel = jnp.where(adm, jnp.where(key_ref[c] >= tb, 1.0, 0.0), 0.0)
        am_ref[c] = jnp.where(sel > 0.0, 0.0, NEG)
        return acc + sel

    n_sel = jnp.sum(lax.fori_loop(0, nck, mask_chunk, jnp.zeros((Q_BLK, CK), F32)),
                    axis=1, keepdims=True)

    @pl.when(jnp.max(n_sel) > TOPK)
    def _():
        def gt_chunk(c, acc):
            adm = (cols0 + c * CK) <= rows
            return acc + jnp.where(adm, jnp.where(key_ref[c] > tb, 1.0, 0.0), 0.0)

        n_gt = jnp.sum(lax.fori_loop(0, nck, gt_chunk, jnp.zeros((Q_BLK, CK), F32)),
                       axis=1, keepdims=True)
        keep = TOPK - n_gt

        def tie_chunk(c, seen):
            adm = (cols0 + c * CK) <= rows
            key = key_ref[c]
            eq = jnp.where(adm, jnp.where(key == tb, 1.0, 0.0), 0.0)
            rank = seen + _dot(eq.astype(BF16), utri_ref[...])
            sel = jnp.where(adm, jnp.where(key > tb, 1.0, 0.0), 0.0)
            sel = jnp.where(eq > 0.0, jnp.where(rank <= keep, 1.0, 0.0), sel)
            am_ref[c] = jnp.where(sel > 0.0, 0.0, NEG)
            return seen + jnp.sum(eq, axis=1, keepdims=True)

        lax.fori_loop(0, nck, tie_chunk, jnp.zeros((Q_BLK, 1), F32))

    for n in range(H_A):
        g = n // G_A
        qn = qa_ref[:, n * HEAD_DIM:(n + 1) * HEAD_DIM]

        def att_chunk(c, carry, n=n, g=g, qn=qn):
            m, l, acc = carry
            start = pl.multiple_of(c * CK, CK)
            kc = ka_ref[pl.ds(start, CK), g * HEAD_DIM:(g + 1) * HEAD_DIM]
            vc = va_ref[pl.ds(start, CK), g * HEAD_DIM:(g + 1) * HEAD_DIM]
            d0 = jnp.clip(j - 2 * c, 0, 2)
            d1 = jnp.clip(j - 2 * c - 1, 0, 2)
            bias = jnp.concatenate([bias_ref[n, d0], bias_ref[n, d1]], axis=1)
            lg = _dot_nt(qn, kc) * ATT_SCALE + bias + am_ref[c]
            m_new = jnp.maximum(m, jnp.max(lg, axis=1, keepdims=True))
            alpha = jnp.exp(m - m_new)
            p = jnp.exp(lg - m_new)
            l = alpha * l + jnp.sum(p, axis=1, keepdims=True)
            acc = alpha * acc + _dot(p.astype(BF16), vc)
            return m_new, l, acc

        m, l, acc = lax.fori_loop(
            0, nck, att_chunk,
            (jnp.full((Q_BLK, 1), -jnp.inf, F32), jnp.zeros((Q_BLK, 1), F32),
             jnp.zeros((Q_BLK, HEAD_DIM), F32)))
        o_ref[:, n * HEAD_DIM:(n + 1) * HEAD_DIM] = acc / l


def _dsa_prompt(q_all, wi_all, ki2, kv_bf, bias_tiles, utri):
    rb = SEQ // Q_BLK
    return pl.pallas_call(
        _dsa_prompt_kernel,
        grid=(BATCH, N_QBLK),
        in_specs=[
            pl.BlockSpec((Q_BLK, W_A), lambda b, j: (b * rb + j, 0)),
            pl.BlockSpec((Q_BLK, H_I * D_IDX), lambda b, j: (b * rb + j, 1)),
            pl.BlockSpec((Q_BLK, LANES), lambda b, j: (b * rb + j, 0)),
            pl.BlockSpec((SEQ, LANES), lambda b, j: (b, 0)),
            pl.BlockSpec((SEQ, KV_A * HEAD_DIM), lambda b, j: (b, 0)),
            pl.BlockSpec((SEQ, KV_A * HEAD_DIM), lambda b, j: (b, 1)),
            pl.BlockSpec((H_A, 3, Q_BLK, LANES), lambda b, j: (0, 0, 0, 0)),
            pl.BlockSpec((CK, CK), lambda b, j: (0, 0)),
        ],
        out_specs=pl.BlockSpec((Q_BLK, W_A), lambda b, j: (b * rb + j, 0)),
        out_shape=jax.ShapeDtypeStruct((M_PROMPT, W_A), F32),
        scratch_shapes=[pltpu.VMEM((H_I, Q_BLK, LANES), BF16),
                        pltpu.VMEM((SEQ // CK, Q_BLK, CK), I32),
                        pltpu.VMEM((SEQ // CK, Q_BLK, CK), F32)],
        compiler_params=_cparams(2),
        name="dsa_prompt",
    )(q_all, q_all, wi_all, ki2, kv_bf, kv_bf, bias_tiles, utri)


def _sb_prompt_kernel(q_ref, k_ref, v_ref, u_ref, o_ref):
    j = pl.program_id(2)
    nck = j // 2 + 1
    q = q_ref[...]
    rows = j * Q_BLK + lax.broadcasted_iota(I32, (Q_BLK, CK), 0)
    cols0 = lax.broadcasted_iota(I32, (Q_BLK, CK), 1)

    def body(i, carry):
        tot, acc = carry
        c = nck - 1 - i
        start = pl.multiple_of(c * CK, CK)
        z = _dot_nt(q, k_ref[pl.ds(start, CK), :]) * ATT_SCALE
        vis = (cols0 + c * CK) < rows
        sp = _softplus(z)
        lm = jnp.where(vis, -sp, 0.0)
        suf = _split_dot(lm, u_ref[...]) + tot
        a = jnp.where(vis, jnp.exp(z - sp + suf), 0.0)
        acc = acc + _dot(a.astype(BF16), v_ref[pl.ds(start, CK), :])
        return tot + jnp.sum(lm, axis=1, keepdims=True), acc

    _, acc = lax.fori_loop(0, nck, body,
                           (jnp.zeros((Q_BLK, 1), F32), jnp.zeros((Q_BLK, HEAD_DIM), F32)))
    o_ref[...] = acc


def _sb_prompt(q_all, kv_bf, ustrict):
    rb = SEQ // Q_BLK
    qb_col0 = (W_A + H_I * D_IDX) // HEAD_DIM
    kb_col0 = 2 * KV_A
    vb_col0 = kb_col0 + H_B
    return pl.pallas_call(
        _sb_prompt_kernel,
        grid=(BATCH, H_B, N_QBLK),
        in_specs=[
            pl.BlockSpec((Q_BLK, HEAD_DIM), lambda b, h, j: (b * rb + j, qb_col0 + h)),
            pl.BlockSpec((SEQ, HEAD_DIM), lambda b, h, j: (b, kb_col0 + h)),
            pl.BlockSpec((SEQ, HEAD_DIM), lambda b, h, j: (b, vb_col0 + h)),
            pl.BlockSpec((CK, CK), lambda b, h, j: (0, 0)),
        ],
        out_specs=pl.BlockSpec((Q_BLK, HEAD_DIM), lambda b, h, j: (b * rb + j, h)),
        out_shape=jax.ShapeDtypeStruct((M_PROMPT, W_B), F32),
        compiler_params=_cparams(3),
        name="stick_breaking_prompt",
    )(q_all, kv_bf, kv_bf, ustrict)


def _sel_sample_kernel(pt_ref, qi_ref, wi_ref, *rest):
    page_refs = rest[:N_PAGES]
    knew_ref, dup_ref, utri_ref, am_ref, key_ref = rest[N_PAGES:]
    s = pl.program_id(1)
    rows8 = lax.broadcasted_iota(I32, (SUBLANES, LANES), 0)
    cols8 = lax.broadcasted_iota(I32, (SUBLANES, LANES), 1)
    qi = qi_ref[...]
    wcol = wi_ref[...]
    row0 = pl.multiple_of(s * SUBLANES, SUBLANES)
    for p in range(S_TILES):
        if p < N_PAGES:
            kp = page_refs[p][...].astype(BF16)
            adm = rows8 < DEC_SEQ
        else:
            kp = knew_ref[...]
            adm = (rows8 < DEC_SEQ) & (cols8 <= rows8)
        r = jnp.maximum(_dot_nt(qi, kp), 0.0) * wcol
        sc = jnp.sum(r.reshape(SUBLANES, H_I, LANES), axis=1)
        key_ref[pl.ds(row0, SUBLANES), p * LANES:(p + 1) * LANES] = _sortable(
            jnp.where(adm, sc, -jnp.inf))

    @pl.when(s == SEQ_GROUP - 1)
    def _():
        nrow = SEQ_GROUP * SUBLANES
        rowsg = lax.broadcasted_iota(I32, (nrow, LANES), 0) % SUBLANES
        colsg = lax.broadcasted_iota(I32, (nrow, LANES), 1)
        adm_past = rowsg < DEC_SEQ
        adm_new = adm_past & (colsg <= rowsg)

        def adm_of(p):
            return adm_past if p < N_PAGES else adm_new

        def count_ge(cand):
            cb = jnp.broadcast_to(cand, (nrow, LANES))
            acc = jnp.zeros((nrow, LANES), F32)
            for p in range(S_TILES):
                acc = acc + jnp.where(key_ref[:, p * LANES:(p + 1) * LANES] >= cb, 1.0, 0.0)
            return jnp.sum(acc, axis=1, keepdims=True)

        thr = _topk_threshold(count_ge, nrow)
        tb = jnp.broadcast_to(thr, (nrow, LANES))

        def write(p, sel):
            dup = _dot(sel.astype(BF16), dup_ref[...])
            am_ref[:, p * CK:(p + 1) * CK] = jnp.where(dup > 0.5, 0.0, NEG)

        n_sel = jnp.zeros((nrow, LANES), F32)
        for p in range(S_TILES):
            sel = jnp.where(adm_of(p),
                            jnp.where(key_ref[:, p * LANES:(p + 1) * LANES] >= tb, 1.0, 0.0), 0.0)
            n_sel = n_sel + sel
            write(p, sel)
        n_sel = jnp.sum(n_sel, axis=1, keepdims=True)

        @pl.when(jnp.max(n_sel) > TOPK)
        def _():
            n_gt = jnp.zeros((nrow, LANES), F32)
            for p in range(S_TILES):
                n_gt = n_gt + jnp.where(
                    adm_of(p),
                    jnp.where(key_ref[:, p * LANES:(p + 1) * LANES] > tb, 1.0, 0.0), 0.0)
            keep = TOPK - jnp.sum(n_gt, axis=1, keepdims=True)
            seen = jnp.zeros((nrow, 1), F32)
            for p in range(S_TILES):
                key = key_ref[:, p * LANES:(p + 1) * LANES]
                eq = jnp.where(adm_of(p), jnp.where(key == tb, 1.0, 0.0), 0.0)
                rank = seen + _dot(eq.astype(BF16), utri_ref[...])
                sel = jnp.where(adm_of(p), jnp.where(key > tb, 1.0, 0.0), 0.0)
                sel = jnp.where(eq > 0.0, jnp.where(rank <= keep, 1.0, 0.0), sel)
                write(p, sel)
                seen = seen + jnp.sum(eq, axis=1, keepdims=True)


def _sel_sample(page_table, layer, qi_s, wi_s, cache_kidx, kidx_new, dup, utri):
    def page_map(p):
        return lambda g, s, pt: (pt[g * SEQ_GROUP + s, p], layer, 0, 0)

    seq_map = lambda g, s, pt: (g * SEQ_GROUP + s, 0, 0)
    grid_spec = pltpu.PrefetchScalarGridSpec(
        num_scalar_prefetch=1,
        grid=(N_SEQ_GROUPS, SEQ_GROUP),
        in_specs=[pl.BlockSpec((None, LANES, D_IDX), seq_map),
                  pl.BlockSpec((None, LANES, 1), seq_map)]
        + [pl.BlockSpec((None, None, PAGE_SIZE, D_IDX), page_map(p)) for p in range(N_PAGES)]
        + [pl.BlockSpec((None, LANES, D_IDX), seq_map),
           pl.BlockSpec((LANES, CK), lambda g, s, pt: (0, 0)),
           pl.BlockSpec((LANES, LANES), lambda g, s, pt: (0, 0))],
        out_specs=pl.BlockSpec((None, SEQ_GROUP * SUBLANES, S_TILES * CK),
                               lambda g, s, pt: (g, 0, 0)),
        scratch_shapes=[pltpu.VMEM((SEQ_GROUP * SUBLANES, S_TILES * LANES), I32)],
    )
    return pl.pallas_call(
        _sel_sample_kernel,
        grid_spec=grid_spec,
        out_shape=jax.ShapeDtypeStruct((N_SEQ_GROUPS, SEQ_GROUP * SUBLANES, S_TILES * CK), F32),
        compiler_params=_cparams(2),
        name="indexer_select_sample",
    )(page_table, qi_s, wi_s, *([cache_kidx] * N_PAGES), kidx_new, dup, utri)


def _decode_kernel(pt_ref, qb_ref, qa_ref, kbn_ref, vbn_ref, kan_ref, van_ref, am_ref, amn_ref,
                   strip_ref, far_ref, u_ref, jm_ref, *rest):
    P = PAGES_PER_STEP
    kb_refs, vb_refs = rest[0:P], rest[P:2 * P]
    ka_refs, va_refs = rest[2 * P:3 * P], rest[3 * P:4 * P]
    ob_ref, oa_ref, accb_ref, carry_ref, m_ref, l_ref, acca_ref = rest[4 * P:]
    ci = pl.program_id(1)
    nrow = DEC_SEQ * H_B

    rows = lax.broadcasted_iota(I32, (nrow, CK), 0)
    lan = lax.broadcasted_iota(I32, (nrow, CK), 1)
    head_b = (lan % H_B) == (rows % H_B)
    group_a = (lan % KV_A) == ((rows % H_A) // G_A)
    qb = qb_ref[...]
    qa = qa_ref[...]

    def am_rows(am8):
        return jnp.concatenate(
            [jnp.broadcast_to(am8[q:q + 1, :], (H_A, am8.shape[1])) for q in range(DEC_SEQ)], axis=0)

    def dsa_update(lg, v2):
        m = m_ref[...]
        m_new = jnp.maximum(m, jnp.max(lg, axis=1, keepdims=True))
        alpha = jnp.exp(m - m_new)
        p = jnp.exp(lg - m_new)
        l_ref[...] = alpha * l_ref[...] + jnp.sum(p, axis=1, keepdims=True)
        acca_ref[...] = alpha * acca_ref[...] + _dot(p.astype(BF16), v2)
        m_ref[...] = m_new

    def sb_block(z, vis, u, jm):
        sp = _softplus(z)
        lm = jnp.where(vis, -sp, 0.0)
        w = z.shape[1]
        carry = carry_ref[:, :w]
        suf = _split_dot(lm, u) + carry
        a = jnp.where(vis, jnp.exp(z - sp + suf), 0.0)
        return a, _split_dot(lm, jm)

    @pl.when(ci == 0)
    def _():
        m_ref[...] = jnp.full((nrow, 1), -jnp.inf, F32)
        l_ref[...] = jnp.zeros((nrow, 1), F32)
        acca_ref[...] = jnp.zeros((nrow, HEAD_DIM), F32)
        carry_ref[...] = jnp.zeros((nrow, CK), F32)
        rows1 = lax.broadcasted_iota(I32, (nrow, LANES), 0)
        lan1 = lax.broadcasted_iota(I32, (nrow, LANES), 1)
        head_b1 = (lan1 % H_B) == (rows1 % H_B)
        group_a1 = (lan1 % KV_A) == ((rows1 % H_A) // G_A)
        vis = head_b1 & ((lan1 // H_B) < (rows1 // H_B))
        z = _dot_nt(qb, kbn_ref[...]) * ATT_SCALE
        a, tot = sb_block(z, vis, u_ref[:LANES, :LANES], jm_ref[:LANES, :LANES])
        accb_ref[...] = _dot(a.astype(BF16), vbn_ref[...])
        carry_ref[...] = jnp.concatenate([tot, tot], axis=1)
        lg = _dot_nt(qa, kan_ref[...]) * ATT_SCALE + strip_ref[:, CK:CK + LANES] \
            + am_rows(amn_ref[...])[:, :LANES]
        dsa_update(jnp.where(group_a1, lg, NEG), van_ref[...])

    far = far_ref[...][:, :1]
    for i in range(P):
        kb2 = kb_refs[i][...].astype(BF16)
        vb2 = vb_refs[i][...].astype(BF16)
        z = _dot_nt(qb, kb2) * ATT_SCALE
        nsub = PAGE_SIZE * H_B // CK
        a_parts = [None] * nsub
        for sb in reversed(range(nsub)):
            a, tot = sb_block(z[:, sb * CK:(sb + 1) * CK], head_b, u_ref[...], jm_ref[...])
            carry_ref[...] = carry_ref[...] + tot
            a_parts[sb] = a.astype(BF16)
        accb_ref[...] = accb_ref[...] + _dot(jnp.concatenate(a_parts, axis=1), vb2)

        ka2 = ka_refs[i][...].astype(BF16)
        va2 = va_refs[i][...].astype(BF16)
        lane0 = (P - 1 - i) * CK
        am = am_rows(am_ref[:, lane0:lane0 + CK])
        if i == 0:
            bias = jnp.where(ci == 0, strip_ref[:, :CK], jnp.broadcast_to(far, (nrow, CK)))
        else:
            bias = far
        lg = _dot_nt(qa, ka2) * ATT_SCALE + bias + am
        dsa_update(jnp.where(group_a, lg, NEG), va2)

    @pl.when(ci == N_PAGE_STEPS - 1)
    def _():
        ob_ref[...] = accb_ref[...]
        oa_ref[...] = acca_ref[...] / l_ref[...]


def _decode(page_table, layer, qb_s, qa_s, kb_new, vb_new, ka_new, va_new, am_s, strip, far,
            u_b, j_b, ckb, cvb, cka, cva):
    P = PAGES_PER_STEP

    def page_map(i, ndim):
        def f(s, ci, pt):
            return (pt[s, N_PAGES - 1 - (P * ci + i)], layer) + (0,) * (ndim - 2)
        return f

    seq_map = lambda s, ci, pt: (s, 0, 0)
    const2 = lambda s, ci, pt: (0, 0)
    nrow = DEC_SEQ * H_B
    rows_b = PAGE_SIZE * H_B
    rows_a = PAGE_SIZE * KV_A
    in_specs = [
        pl.BlockSpec((None, nrow, HEAD_DIM), seq_map),
        pl.BlockSpec((None, nrow, HEAD_DIM), seq_map),
        pl.BlockSpec((None, LANES, HEAD_DIM), seq_map),
        pl.BlockSpec((None, LANES, HEAD_DIM), seq_map),
        pl.BlockSpec((None, LANES, HEAD_DIM), seq_map),
        pl.BlockSpec((None, LANES, HEAD_DIM), seq_map),
        pl.BlockSpec((None, SUBLANES, P * CK), lambda s, ci, pt: (s, 0, N_PAGE_STEPS - 1 - ci)),
        pl.BlockSpec((None, SUBLANES, CK), lambda s, ci, pt: (s, 0, N_PAGES)),
        pl.BlockSpec((nrow, 2 * CK), const2),
        pl.BlockSpec((nrow, LANES), const2),
        pl.BlockSpec((CK, CK), const2),
        pl.BlockSpec((CK, CK), const2),
    ]
    in_specs += [pl.BlockSpec((None, None, rows_b, HEAD_DIM), page_map(i, 4)) for i in range(P)] * 2
    in_specs += [pl.BlockSpec((None, None, rows_a, HEAD_DIM), page_map(i, 4)) for i in range(P)] * 2
    grid_spec = pltpu.PrefetchScalarGridSpec(
        num_scalar_prefetch=1,
        grid=(DEC_BATCH, N_PAGE_STEPS),
        in_specs=in_specs,
        out_specs=[pl.BlockSpec((None, nrow, HEAD_DIM), seq_map)] * 2,
        scratch_shapes=[pltpu.VMEM((nrow, HEAD_DIM), F32), pltpu.VMEM((nrow, CK), F32),
                        pltpu.VMEM((nrow, 1), F32), pltpu.VMEM((nrow, 1), F32),
                        pltpu.VMEM((nrow, HEAD_DIM), F32)],
    )
    return pl.pallas_call(
        _decode_kernel,
        grid_spec=grid_spec,
        out_shape=[jax.ShapeDtypeStruct((DEC_BATCH, nrow, HEAD_DIM), F32)] * 2,
        compiler_params=_cparams(2),
        name="decode_attention_sample",
    )(page_table, qb_s, qa_s, kb_new, vb_new, ka_new, va_new, am_s, am_s, strip, far, u_b, j_b,
      *([ckb] * P), *([cvb] * P), *([cka] * P), *([cva] * P))


def _merge_kernel(oa_ref, ga_ref, ob_ref, gb_ref, ma_ref, mb_ref, wpa_ref, wpb_ref, u_ref):
    ga = ga_ref[...]
    gb = gb_ref[...]
    ya = _dot((oa_ref[...] * (ga * jax.nn.sigmoid(ga))).astype(BF16), wpa_ref[...])
    yb = _dot((ob_ref[...] * (gb * jax.nn.sigmoid(gb))).astype(BF16), wpb_ref[...])
    u = jax.nn.sigmoid(ma_ref[...]) * ya + jax.nn.sigmoid(mb_ref[...]) * yb
    u_ref[...] = u.astype(BF16)


def _merge(o_a, o_b, gates, w_pa, w_pb, tm=256):
    return pl.pallas_call(
        _merge_kernel,
        grid=(M_ALL // tm,),
        in_specs=[pl.BlockSpec((tm, W_A), lambda i: (i, 0)),
                  pl.BlockSpec((tm, W_A), lambda i: (i, 0)),
                  pl.BlockSpec((tm, W_B), lambda i: (i, 0)),
                  pl.BlockSpec((tm, W_B), lambda i: (i, 1)),
                  pl.BlockSpec((tm, D_MODEL), lambda i: (i, 1)),
                  pl.BlockSpec((tm, D_MODEL), lambda i: (i, 2)),
                  pl.BlockSpec((W_A, D_MODEL), lambda i: (0, 0)),
                  pl.BlockSpec((W_B, D_MODEL), lambda i: (0, 0))],
        out_specs=pl.BlockSpec((tm, D_MODEL), lambda i: (i, 0)),
        out_shape=jax.ShapeDtypeStruct((M_ALL, D_MODEL), BF16),
        compiler_params=_cparams(1),
        name="gated_merge",
    )(o_a, gates, o_b, gates, gates, gates, w_pa, w_pb)


def _outproj_kernel(u_ref, x_ref, w_ref, g_ref, b_ref, o_ref):
    mix = _dot(u_ref[...], w_ref[...])
    o_ref[...] = _layer_norm_rows(ALPHA * x_ref[...] + mix, g_ref[...], b_ref[...])


def _outproj(u, x, w_out, g, b, tm=256):
    return pl.pallas_call(
        _outproj_kernel,
        grid=(M_ALL // tm,),
        in_specs=[pl.BlockSpec((tm, D_MODEL), lambda i: (i, 0)),
                  pl.BlockSpec((tm, D_MODEL), lambda i: (i, 0)),
                  pl.BlockSpec((D_MODEL, D_MODEL), lambda i: (0, 0)),
                  pl.BlockSpec((1, D_MODEL), lambda i: (0, 0)),
                  pl.BlockSpec((1, D_MODEL), lambda i: (0, 0))],
        out_specs=pl.BlockSpec((tm, D_MODEL), lambda i: (i, 0)),
        out_shape=jax.ShapeDtypeStruct((M_ALL, D_MODEL), F32),
        compiler_params=_cparams(1),
        name="out_proj_norm",
    )(u, x, w_out, g, b)


def _ple_kernel(x_ref, p_ref, wg_ref, we_ref, g_ref, b_ref, o_ref):
    x = x_ref[...]
    gate = jax.nn.sigmoid(_dot(x.astype(BF16), wg_ref[...]))
    ple = gate * _dot(p_ref[...].astype(BF16), we_ref[...])
    o_ref[...] = _layer_norm_rows(ALPHA * x + ple, g_ref[...], b_ref[...])


def _ple(x, p, w_pg, w_pe, g, b, tm=256):
    return pl.pallas_call(
        _ple_kernel,
        grid=(M_ALL // tm,),
        in_specs=[pl.BlockSpec((tm, D_MODEL), lambda i: (i, 0)),
                  pl.BlockSpec((tm, PLE_DIM), lambda i: (i, 0)),
                  pl.BlockSpec((D_MODEL, D_MODEL), lambda i: (0, 0)),
                  pl.BlockSpec((PLE_DIM, D_MODEL), lambda i: (0, 0)),
                  pl.BlockSpec((1, D_MODEL), lambda i: (0, 0)),
                  pl.BlockSpec((1, D_MODEL), lambda i: (0, 0))],
        out_specs=pl.BlockSpec((tm, D_MODEL), lambda i: (i, 0)),
        out_shape=jax.ShapeDtypeStruct((M_ALL, D_MODEL), F32),
        compiler_params=_cparams(1),
        name="ple_update_norm",
    )(x, p, w_pg, w_pe, g, b)


def _tri_constants():
    a = jnp.arange(CK, dtype=I32)
    ustrict = (a[:, None] > a[None, :]).astype(BF16)
    utri = (a[:, None] <= a[None, :]).astype(BF16)
    same_head = (a[:, None] % H_B) == (a[None, :] % H_B)
    u_b = (same_head & ((a[:, None] // H_B) > (a[None, :] // H_B))).astype(BF16)
    j_b = same_head.astype(BF16)
    s = jnp.arange(LANES, dtype=I32)
    dup = (s[:, None] == (a[None, :] // KV_A)).astype(BF16)
    return ustrict, utri, u_b, j_b, dup


def kernel(x_prompt, x_sample, cache_k_a, cache_v_a, cache_kidx, cache_k_b, cache_v_b, page_table,
           p_prompt, p_sample, w_in, ln_kidx_g, ln_kidx_b, rel_bias, w_pa, w_pb, w_out,
           ln1_g, ln1_b, w_pe, w_pg, ln2_g, ln2_b):
    n_pool = cache_k_a.shape[0]
    ckb = cache_k_b.reshape(n_pool, DEPTH, PAGE_SIZE * H_B, HEAD_DIM)
    cvb = cache_v_b.reshape(n_pool, DEPTH, PAGE_SIZE * H_B, HEAD_DIM)
    cka = cache_k_a.reshape(n_pool, DEPTH, PAGE_SIZE * KV_A, HEAD_DIM)
    cva = cache_v_a.reshape(n_pool, DEPTH, PAGE_SIZE * KV_A, HEAD_DIM)

    ustrict, utri, u_b, j_b, dup = _tri_constants()
    bias_tiles, strip, far = _bias_tables(rel_bias)

    o_qa, o_ka, o_va, o_ga = 0, W_A, W_A + 256, W_A + 512
    o_qi = o_ga + W_A
    o_ki = o_qi + H_I * D_IDX
    o_wi = o_ki + D_IDX
    o_qb = o_wi + H_I
    o_kb, o_vb, o_gb = o_qb + W_B, o_qb + 2 * W_B, o_qb + 3 * W_B
    o_ma = o_gb + W_B
    o_mb = o_ma + D_MODEL

    x = jnp.concatenate([x_prompt.reshape(M_PROMPT, D_MODEL), x_sample.reshape(M_SAMPLE, D_MODEL)], 0)
    p_all = jnp.concatenate([p_prompt.reshape(DEPTH, M_PROMPT, PLE_DIM),
                             p_sample.reshape(DEPTH, M_SAMPLE, PLE_DIM)], axis=1)

    rows_p = {n: [] for n in ("k_a", "v_a", "kidx", "k_b", "v_b")}
    rows_s = {n: [] for n in ("k_a", "v_a", "kidx", "k_b", "v_b")}

    for i in range(DEPTH):
        w = w_in[i]
        cols = lambda a, n: w[:, a:a + n].astype(BF16)
        w_q = jnp.concatenate([cols(o_qa, W_A), cols(o_qi, H_I * D_IDX), cols(o_qb, W_B)], axis=1)
        w_kv = jnp.concatenate([cols(o_ka, 256), cols(o_va, 256), cols(o_kb, W_B), cols(o_vb, W_B)], axis=1)
        w_g = jnp.concatenate([cols(o_ga, W_A), cols(o_gb, W_B), cols(o_ma, D_MODEL), cols(o_mb, D_MODEL)], axis=1)
        w_kw = jnp.concatenate([cols(o_ki, D_IDX), cols(o_ki, D_IDX), cols(o_wi, H_I),
                                jnp.zeros((D_MODEL, LANES - H_I), BF16)], axis=1)
        g2 = jnp.concatenate([ln_kidx_g[i], ln_kidx_g[i]])[None, :]
        b2 = jnp.concatenate([ln_kidx_b[i], ln_kidx_b[i]])[None, :]

        xb = x.astype(BF16)
        (q_all,) = _matmul(xb, w_q, [BF16])
        kv_f, kv_bf = _matmul(xb, w_kv, [F32, BF16])
        (gates,) = _matmul(xb, w_g, [F32])
        kidx2, ki2_bf, wi_all = _kiw(xb, w_kw, g2, b2)

        kvp, kvs = kv_f[:M_PROMPT], kv_f[M_PROMPT:]
        rows_p["k_a"].append(kvp[:, 0:256].reshape(BATCH, SEQ, KV_A, HEAD_DIM))
        rows_p["v_a"].append(kvp[:, 256:512].reshape(BATCH, SEQ, KV_A, HEAD_DIM))
        rows_p["k_b"].append(kvp[:, 512:512 + W_B].reshape(BATCH, SEQ, H_B, HEAD_DIM))
        rows_p["v_b"].append(kvp[:, 512 + W_B:].reshape(BATCH, SEQ, H_B, HEAD_DIM))
        rows_p["kidx"].append(kidx2[:M_PROMPT, :D_IDX].reshape(BATCH, SEQ, D_IDX))
        rows_s["k_a"].append(kvs[:, 0:256].reshape(DEC_BATCH, DEC_SEQ, KV_A, HEAD_DIM))
        rows_s["v_a"].append(kvs[:, 256:512].reshape(DEC_BATCH, DEC_SEQ, KV_A, HEAD_DIM))
        rows_s["k_b"].append(kvs[:, 512:512 + W_B].reshape(DEC_BATCH, DEC_SEQ, H_B, HEAD_DIM))
        rows_s["v_b"].append(kvs[:, 512 + W_B:].reshape(DEC_BATCH, DEC_SEQ, H_B, HEAD_DIM))
        rows_s["kidx"].append(kidx2[M_PROMPT:, :D_IDX].reshape(DEC_BATCH, DEC_SEQ, D_IDX))

        oa_p = _dsa_prompt(q_all, wi_all, ki2_bf, kv_bf, bias_tiles, utri)
        ob_p = _sb_prompt(q_all, kv_bf, ustrict)

        qs = q_all[M_PROMPT:]
        kvs_bf = kv_bf[M_PROMPT:]
        qa_s = qs[:, :W_A].reshape(DEC_BATCH, DEC_SEQ * H_A, HEAD_DIM)
        qb_s = qs[:, W_A + H_I * D_IDX:].reshape(DEC_BATCH, DEC_SEQ * H_B, HEAD_DIM)
        qi_s = qs[:, W_A:W_A + H_I * D_IDX].reshape(DEC_BATCH, DEC_SEQ * H_I, D_IDX)
        qi_s = jnp.pad(qi_s, ((0, 0), (0, LANES - DEC_SEQ * H_I), (0, 0)))
        wi_s = wi_all[M_PROMPT:, :H_I].reshape(DEC_BATCH, DEC_SEQ * H_I, 1)
        wi_s = jnp.pad(wi_s, ((0, 0), (0, LANES - DEC_SEQ * H_I), (0, 0)))
        kidx_new = ki2_bf[M_PROMPT:, :D_IDX].reshape(DEC_BATCH, DEC_SEQ, D_IDX)
        kidx_new = jnp.pad(kidx_new, ((0, 0), (0, LANES - DEC_SEQ), (0, 0)))

        def new_tile(a, heads, slots):
            a = a.reshape(DEC_BATCH, DEC_SEQ, heads, HEAD_DIM)
            a = jnp.pad(a, ((0, 0), (0, slots - DEC_SEQ), (0, 0), (0, 0)))
            return a.reshape(DEC_BATCH, slots * heads, HEAD_DIM)

        ka_new = new_tile(kvs_bf[:, 0:256], KV_A, NEW_A)
        va_new = new_tile(kvs_bf[:, 256:512], KV_A, NEW_A)
        kb_new = new_tile(kvs_bf[:, 512:512 + W_B], H_B, NEW_B)
        vb_new = new_tile(kvs_bf[:, 512 + W_B:], H_B, NEW_B)

        am = _sel_sample(page_table, i, qi_s, wi_s, cache_kidx, kidx_new, dup, utri[:LANES, :LANES])
        am_s = am.reshape(DEC_BATCH, SUBLANES, S_TILES * CK)
        ob_s, oa_s = _decode(page_table, i, qb_s, qa_s, kb_new, vb_new, ka_new, va_new, am_s,
                             strip, far, u_b, j_b, ckb, cvb, cka, cva)

        o_a = jnp.concatenate([oa_p, oa_s.reshape(M_SAMPLE, W_A)], axis=0)
        o_b = jnp.concatenate([ob_p, ob_s.reshape(M_SAMPLE, W_B)], axis=0)

        u = _merge(o_a, o_b, gates, w_pa[i].astype(BF16), w_pb[i].astype(BF16))
        x = _outproj(u, x, w_out[i].astype(BF16), ln1_g[i][None, :], ln1_b[i][None, :])
        x = _ple(x, p_all[i], w_pg[i].astype(BF16), w_pe[i].astype(BF16),
                 ln2_g[i][None, :], ln2_b[i][None, :])

    st = lambda rows: jnp.stack(rows, axis=1)
    return (x[:M_PROMPT].reshape(BATCH, SEQ, D_MODEL), x[M_PROMPT:].reshape(DEC_BATCH, DEC_SEQ, D_MODEL),
            st(rows_p["k_a"]), st(rows_p["v_a"]), st(rows_p["kidx"]), st(rows_p["k_b"]), st(rows_p["v_b"]),
            st(rows_s["k_a"]), st(rows_s["v_a"]), st(rows_s["kidx"]), st(rows_s["k_b"]), st(rows_s["v_b"]))
```

```python
import functools
import math

import jax
import jax.numpy as jnp
from jax import lax
from jax.experimental import pallas as pl
from jax.experimental.pallas import tpu as pltpu

F32 = jnp.float32
BF16 = jnp.bfloat16
I32 = jnp.int32

D_MODEL = 2048
BATCH = 4
SEQ = 2048
DEPTH = 2
DEC_BATCH = 128
DEC_SEQ = 4
PAST_LEN = 2048
PAGE_SIZE = 128
N_PAGES = PAST_LEN // PAGE_SIZE
HEAD_DIM = 128
H_A = 8
KV_A = 2
G_A = H_A // KV_A
W_A = H_A * HEAD_DIM
H_I = 16
D_IDX = 64
TOPK = 256
H_B = 8
W_B = H_B * HEAD_DIM
N_BUCKETS = 32
MAX_DISTANCE = 128
PLE_DIM = 256
LN_EPS = 1e-5
ALPHA = (2 * DEPTH) ** 0.25

M_PROMPT = BATCH * SEQ
M_SAMPLE = DEC_BATCH * DEC_SEQ
M_ALL = M_PROMPT + M_SAMPLE

LANES = 128
SUBLANES = 8
Q_BLK = 128
CK = 256
N_QBLK = SEQ // Q_BLK
NEG = -1e30
INT_MIN = -2 ** 31
ATT_SCALE = HEAD_DIM ** -0.5
VMEM_LIMIT = 52 * 1024 * 1024

SEQ_GROUP = 32
N_SEQ_GROUPS = DEC_BATCH // SEQ_GROUP
S_TILES = N_PAGES + 1
PAGES_PER_STEP = 8
N_PAGE_STEPS = N_PAGES // PAGES_PER_STEP
SB_HEADS_PER_STEP = 4
SEARCH_BITS_PER_CHECK = 4
NEW_B = LANES // H_B
NEW_A = LANES // KV_A


def _cparams(n_axes):
    return pltpu.CompilerParams(dimension_semantics=("arbitrary",) * n_axes,
                                vmem_limit_bytes=VMEM_LIMIT)


def _dot_nt(a, b):
    return lax.dot_general(a, b, (((1,), (1,)), ((), ())), preferred_element_type=F32)


def _dot(a, b):
    return jnp.dot(a, b, preferred_element_type=F32)


def _sortable(s):
    b = pltpu.bitcast(s, I32)
    return jnp.where(b < 0, b ^ jnp.int32(0x7FFFFFFF), b)


def _hi_lo(x):
    hi = x.astype(BF16)
    lo = (x - hi.astype(F32)).astype(BF16)
    return jnp.concatenate([hi, lo], axis=1)


def _softplus(z):
    return jnp.maximum(z, 0.0) + jnp.log(1.0 + jnp.exp(-jnp.abs(z)))


def _layer_norm_rows(x, g, b):
    mu = jnp.mean(x, axis=-1, keepdims=True)
    xc = x - mu
    var = jnp.mean(xc * xc, axis=-1, keepdims=True)
    return xc * lax.rsqrt(var + LN_EPS) * g + b


def _mm_kernel(x_ref, w_ref, *o_refs):
    acc = _dot(x_ref[...], w_ref[...])
    for o in o_refs:
        o[...] = acc.astype(o.dtype)


def _matmul(x, w, out_dtypes, tm=512, tn=512):
    m, k = x.shape
    n = w.shape[1]
    return pl.pallas_call(
        _mm_kernel,
        grid=(n // tn, m // tm),
        in_specs=[pl.BlockSpec((tm, k), lambda j, i: (i, 0)),
                  pl.BlockSpec((k, tn), lambda j, i: (0, j))],
        out_specs=[pl.BlockSpec((tm, tn), lambda j, i: (i, j)) for _ in out_dtypes],
        out_shape=[jax.ShapeDtypeStruct((m, n), d) for d in out_dtypes],
        compiler_params=_cparams(2),
        name="proj_matmul",
    )(x, w)


def _kiw_kernel(x_ref, w_ref, g_ref, b_ref, kf_ref, kb_ref, wi_ref):
    h = _dot(x_ref[...], w_ref[...])
    k2 = _layer_norm_rows(h[:, :LANES], g_ref[...], b_ref[...])
    kf_ref[...] = k2
    kb_ref[...] = k2.astype(BF16)
    wi_ref[...] = h[:, LANES:] * (H_I ** -0.5 * D_IDX ** -0.5)


def _kiw(x, w4, g2, b2, tm=512):
    m, k = x.shape
    return pl.pallas_call(
        _kiw_kernel,
        grid=(m // tm,),
        in_specs=[pl.BlockSpec((tm, k), lambda i: (i, 0)),
                  pl.BlockSpec((k, 2 * LANES), lambda i: (0, 0)),
                  pl.BlockSpec((1, LANES), lambda i: (0, 0)),
                  pl.BlockSpec((1, LANES), lambda i: (0, 0))],
        out_specs=[pl.BlockSpec((tm, LANES), lambda i: (i, 0))] * 3,
        out_shape=[jax.ShapeDtypeStruct((m, LANES), F32),
                   jax.ShapeDtypeStruct((m, LANES), BF16),
                   jax.ShapeDtypeStruct((m, LANES), F32)],
        compiler_params=_cparams(1),
        name="indexer_key_proj",
    )(x, w4, g2, b2)


def _t5_bucket(d):
    n = jnp.maximum(d, 0)
    max_exact = N_BUCKETS // 2
    nf = jnp.maximum(n, 1).astype(F32)
    large = max_exact + (jnp.log(nf / max_exact) / math.log(MAX_DISTANCE / max_exact)
                         * (N_BUCKETS - max_exact)).astype(I32)
    return jnp.where(n < max_exact, n, jnp.minimum(large, N_BUCKETS - 1))


def _bias_kernel(rb_ref, tiles_ref, strip_ref, far_ref):
    ii = lax.broadcasted_iota(I32, (Q_BLK, LANES), 0)
    jj = lax.broadcasted_iota(I32, (Q_BLK, LANES), 1)
    bk0 = _t5_bucket(ii - jj)
    bk1 = _t5_bucket(ii - jj + LANES)
    for n in range(H_A):
        t0 = jnp.zeros((Q_BLK, LANES), F32)
        t1 = jnp.zeros((Q_BLK, LANES), F32)
        for b in range(N_BUCKETS):
            t0 = jnp.where(bk0 == b, rb_ref[b, n], t0)
            t1 = jnp.where(bk1 == b, rb_ref[b, n], t1)
        tiles_ref[n, 0] = t0
        tiles_ref[n, 1] = t1
        tiles_ref[n, 2] = jnp.full((Q_BLK, LANES), rb_ref[N_BUCKETS - 1, n], F32)
    rows = lax.broadcasted_iota(I32, (DEC_SEQ * H_A, 2 * CK), 0)
    lan = lax.broadcasted_iota(I32, (DEC_SEQ * H_A, 2 * CK), 1)
    q = rows // H_A
    n_of_row = rows % H_A
    pos = jnp.where(lan < CK, PAST_LEN - PAGE_SIZE + lan // KV_A, PAST_LEN + (lan - CK) // KV_A)
    bks = _t5_bucket(PAST_LEN + q - pos)
    strip = jnp.zeros((DEC_SEQ * H_A, 2 * CK), F32)
    far = jnp.zeros((DEC_SEQ * H_A, LANES), F32)
    rows_f = lax.broadcasted_iota(I32, (DEC_SEQ * H_A, LANES), 0) % H_A
    for n in range(H_A):
        sn = jnp.zeros((DEC_SEQ * H_A, 2 * CK), F32)
        for b in range(N_BUCKETS):
            sn = jnp.where(bks == b, rb_ref[b, n], sn)
        strip = jnp.where(n_of_row == n, sn, strip)
        far = jnp.where(rows_f == n, rb_ref[N_BUCKETS - 1, n], far)
    strip_ref[...] = strip
    far_ref[...] = far


def _bias_tables(rel_bias):
    return pl.pallas_call(
        _bias_kernel,
        in_specs=[pl.BlockSpec(memory_space=pltpu.SMEM)],
        out_shape=[jax.ShapeDtypeStruct((H_A, 3, Q_BLK, LANES), F32),
                   jax.ShapeDtypeStruct((DEC_SEQ * H_A, 2 * CK), F32),
                   jax.ShapeDtypeStruct((DEC_SEQ * H_A, LANES), F32)],
        name="t5_bias_tables",
    )(rel_bias)


def _topk_threshold(count_ge, n_keys, few):
    rows = few.shape[0]
    t_init = jnp.full((rows, 1), INT_MIN, I32)
    cnt_init = jnp.zeros((rows, 1), F32) + n_keys
    passes_per_check = SEARCH_BITS_PER_CHECK // 2

    def pending(cnt):
        return (jnp.max(jnp.where(few, 0.0, jnp.abs(cnt - TOPK))) > 0.0).astype(I32)

    def steps(state):
        g, t, cnt, _ = state
        for b in range(passes_per_check):
            i = g * passes_per_check + b
            hi = jnp.left_shift(jnp.int32(1), 31 - 2 * i)
            lo = jnp.left_shift(jnp.int32(1), 30 - 2 * i)
            cands = [t + lo, t + hi, t + hi + lo]
            counts = count_ge(cands)
            for cand, c in zip(cands, counts):
                ok = c >= TOPK
                t = jnp.where(ok, cand, t)
                cnt = jnp.where(ok, c, cnt)
        return g + 1, t, cnt, pending(cnt)

    def cond(state):
        g, _, _, flag = state
        return (g < 32 // SEARCH_BITS_PER_CHECK) & (flag > 0)

    _, t, _, _ = lax.while_loop(cond, steps, (jnp.int32(0), t_init, cnt_init, pending(cnt_init)))
    return jnp.where(few, jnp.int32(INT_MIN), t)


def _dsa_prompt_kernel(qa_ref, qi_ref, wi_ref, ki2_ref, ka_ref, va_ref, bias_ref, utri_ref,
                       o_ref, qim_ref, qg_ref, key_ref, am_ref, m_ref, acc_ref):
    j = pl.program_id(1)
    nck = j // 2 + 1
    t0 = j * Q_BLK

    lane = lax.broadcasted_iota(I32, (Q_BLK, LANES), 1)
    for pr in range(H_I // 2):
        qp = qi_ref[:, pr * LANES:(pr + 1) * LANES].astype(F32)
        qim_ref[(2 * pr) * Q_BLK:(2 * pr + 1) * Q_BLK, :] = jnp.where(lane < D_IDX, qp, 0.0).astype(BF16)
        qim_ref[(2 * pr + 1) * Q_BLK:(2 * pr + 2) * Q_BLK, :] = jnp.where(lane >= D_IDX, qp, 0.0).astype(BF16)
    for n in range(H_A):
        g, nl = divmod(n, G_A)
        qg_ref[g, nl * Q_BLK:(nl + 1) * Q_BLK, :] = qa_ref[:, n * HEAD_DIM:(n + 1) * HEAD_DIM]

    rows = t0 + lax.broadcasted_iota(I32, (Q_BLK, CK), 0)
    cols0 = lax.broadcasted_iota(I32, (Q_BLK, CK), 1)
    wi = wi_ref[...]

    def score_chunk(c, carry):
        kc = ki2_ref[pl.ds(pl.multiple_of(c * CK, CK), CK), :]
        d = _dot_nt(qim_ref[...], kc)
        s = jnp.zeros((Q_BLK, CK), F32)
        for h in range(H_I):
            s = s + wi[:, h:h + 1] * jnp.maximum(d[h * Q_BLK:(h + 1) * Q_BLK], 0.0)
        adm = (cols0 + c * CK) <= rows
        key_ref[c] = _sortable(jnp.where(adm, s, -jnp.inf))
        return carry

    lax.fori_loop(0, nck, score_chunk, 0)

    def count_ge(cands):
        cbs = [jnp.broadcast_to(cand, (Q_BLK, LANES)) for cand in cands]

        def body(c, accs):
            k0 = key_ref[c, :, :LANES]
            k1 = key_ref[c, :, LANES:]
            return tuple(acc + jnp.where(k0 >= cb, 1.0, 0.0) + jnp.where(k1 >= cb, 1.0, 0.0)
                         for acc, cb in zip(accs, cbs))

        accs = lax.fori_loop(0, nck, body, tuple(jnp.zeros((Q_BLK, LANES), F32) for _ in cands))
        return [jnp.sum(acc, axis=1, keepdims=True) for acc in accs]

    few = (t0 + lax.broadcasted_iota(I32, (Q_BLK, 1), 0)) < TOPK
    thr = _topk_threshold(count_ge, (nck * CK).astype(F32), few)
    tb = jnp.broadcast_to(thr, (Q_BLK, CK))

    def mask_chunk(c, acc):
        adm = (cols0 + c * CK) <= rows
        sel = jnp.where(adm, jnp.where(key_ref[c] >= tb, 1.0, 0.0), 0.0)
        am_ref[c] = jnp.where(sel > 0.0, 0.0, NEG)
        return acc + sel

    n_sel = jnp.sum(lax.fori_loop(0, nck, mask_chunk, jnp.zeros((Q_BLK, CK), F32)),
                    axis=1, keepdims=True)

    @pl.when(jnp.max(n_sel) > TOPK)
    def _():
        def gt_chunk(c, acc):
            adm = (cols0 + c * CK) <= rows
            return acc + jnp.where(adm, jnp.where(key_ref[c] > tb, 1.0, 0.0), 0.0)

        n_gt = jnp.sum(lax.fori_loop(0, nck, gt_chunk, jnp.zeros((Q_BLK, CK), F32)),
                       axis=1, keepdims=True)
        keep = TOPK - n_gt

        def tie_chunk(c, seen):
            adm = (cols0 + c * CK) <= rows
            key = key_ref[c]
            eq = jnp.where(adm, jnp.where(key == tb, 1.0, 0.0), 0.0)
            rank = seen + _dot(eq.astype(BF16), utri_ref[...])
            sel = jnp.where(adm, jnp.where(key > tb, 1.0, 0.0), 0.0)
            sel = jnp.where(eq > 0.0, jnp.where(rank <= keep, 1.0, 0.0), sel)
            am_ref[c] = jnp.where(sel > 0.0, 0.0, NEG)
            return seen + jnp.sum(eq, axis=1, keepdims=True)

        lax.fori_loop(0, nck, tie_chunk, jnp.zeros((Q_BLK, 1), F32))

    grows = G_A * Q_BLK

    def chunk_logits(c, g):
        start = pl.multiple_of(c * CK, CK)
        d0 = jnp.clip(j - 2 * c, 0, 2)
        d1 = jnp.clip(j - 2 * c - 1, 0, 2)
        am4 = jnp.concatenate([am_ref[c]] * G_A, axis=0)
        kc = ka_ref[pl.ds(start, CK), g * HEAD_DIM:(g + 1) * HEAD_DIM]
        bias = jnp.concatenate(
            [jnp.concatenate([bias_ref[g * G_A + nl, d0], bias_ref[g * G_A + nl, d1]], axis=1)
             for nl in range(G_A)], axis=0)
        return _dot_nt(qg_ref[g], kc) * ATT_SCALE + bias + am4

    def max_chunk(c, ms):
        out = []
        for g in range(KV_A):
            lg = chunk_logits(c, g)
            out.append(jnp.maximum(ms[g], jnp.maximum(lg[:, :LANES], lg[:, LANES:])))
        return tuple(out)

    ms = lax.fori_loop(0, nck, max_chunk,
                       tuple(jnp.full((grows, LANES), -jnp.inf, F32) for _ in range(KV_A)))
    for g in range(KV_A):
        m_ref[g] = jnp.broadcast_to(jnp.max(ms[g], axis=1, keepdims=True), (grows, LANES))
    acc_ref[...] = jnp.zeros((KV_A, grows, 2 * HEAD_DIM), F32)
    ones = jnp.ones((CK, HEAD_DIM), BF16)

    def acc_chunk(c, carry):
        start = pl.multiple_of(c * CK, CK)
        for g in range(KV_A):
            m = m_ref[g]
            p = jnp.exp(chunk_logits(c, g) - jnp.concatenate([m, m], axis=1))
            v1 = jnp.concatenate([va_ref[pl.ds(start, CK), g * HEAD_DIM:(g + 1) * HEAD_DIM], ones], axis=1)
            acc_ref[g] = acc_ref[g] + _dot(p.astype(BF16), v1)
        return carry

    lax.fori_loop(0, nck, acc_chunk, 0)
    for n in range(H_A):
        g, nl = divmod(n, G_A)
        acc = acc_ref[g, nl * Q_BLK:(nl + 1) * Q_BLK, :]
        o_ref[:, n * HEAD_DIM:(n + 1) * HEAD_DIM] = acc[:, :HEAD_DIM] / acc[:, HEAD_DIM:]


def _dsa_prompt(q_all, wi_all, ki2, kv_bf, bias_tiles, utri):
    rb = SEQ // Q_BLK
    grows = G_A * Q_BLK
    return pl.pallas_call(
        _dsa_prompt_kernel,
        grid=(BATCH, N_QBLK),
        in_specs=[
            pl.BlockSpec((Q_BLK, W_A), lambda b, j: (b * rb + j, 0)),
            pl.BlockSpec((Q_BLK, H_I * D_IDX), lambda b, j: (b * rb + j, 1)),
            pl.BlockSpec((Q_BLK, LANES), lambda b, j: (b * rb + j, 0)),
            pl.BlockSpec((SEQ, LANES), lambda b, j: (b, 0)),
            pl.BlockSpec((SEQ, KV_A * HEAD_DIM), lambda b, j: (b, 0)),
            pl.BlockSpec((SEQ, KV_A * HEAD_DIM), lambda b, j: (b, 1)),
            pl.BlockSpec((H_A, 3, Q_BLK, LANES), lambda b, j: (0, 0, 0, 0)),
            pl.BlockSpec((CK, CK), lambda b, j: (0, 0)),
        ],
        out_specs=pl.BlockSpec((Q_BLK, W_A), lambda b, j: (b * rb + j, 0)),
        out_shape=jax.ShapeDtypeStruct((M_PROMPT, W_A), F32),
        scratch_shapes=[pltpu.VMEM((H_I * Q_BLK, LANES), BF16),
                        pltpu.VMEM((KV_A, grows, HEAD_DIM), BF16),
                        pltpu.VMEM((SEQ // CK, Q_BLK, CK), I32),
                        pltpu.VMEM((SEQ // CK, Q_BLK, CK), F32),
                        pltpu.VMEM((KV_A, grows, LANES), F32),
                        pltpu.VMEM((KV_A, grows, 2 * HEAD_DIM), F32)],
        compiler_params=_cparams(2),
        name="dsa_prompt",
    )(q_all, q_all, wi_all, ki2, kv_bf, kv_bf, bias_tiles, utri)


def _sb_prompt_kernel(q_ref, k_ref, v_ref, u2_ref, o_ref):
    nh = SB_HEADS_PER_STEP
    j = pl.program_id(2)
    nck = j // 2 + 1
    rows = j * Q_BLK + lax.broadcasted_iota(I32, (Q_BLK, CK), 0)
    cols0 = lax.broadcasted_iota(I32, (Q_BLK, CK), 1)

    def body(i, carry):
        tots, accs = carry
        c = nck - 1 - i
        start = pl.multiple_of(c * CK, CK)
        vis = (cols0 + c * CK) < rows
        zs, sps, lms = [], [], []
        for hh in range(nh):
            hs = slice(hh * HEAD_DIM, (hh + 1) * HEAD_DIM)
            z = _dot_nt(q_ref[:, hs], k_ref[pl.ds(start, CK), hs]) * ATT_SCALE
            sp = _softplus(z)
            zs.append(z)
            sps.append(sp)
            lms.append(jnp.where(vis, -sp, 0.0))
        within = _dot(jnp.concatenate([_hi_lo(lm) for lm in lms], axis=0), u2_ref[...])
        new_tots, new_accs = [], []
        for hh in range(nh):
            hs = slice(hh * HEAD_DIM, (hh + 1) * HEAD_DIM)
            suf = within[hh * Q_BLK:(hh + 1) * Q_BLK] + tots[hh]
            a = jnp.where(vis, jnp.exp(zs[hh] - sps[hh] + suf), 0.0)
            new_accs.append(accs[hh] + _dot(a.astype(BF16), v_ref[pl.ds(start, CK), hs]))
            new_tots.append(tots[hh] + jnp.sum(lms[hh], axis=1, keepdims=True))
        return tuple(new_tots), tuple(new_accs)

    init = (tuple(jnp.zeros((Q_BLK, 1), F32) for _ in range(nh)),
            tuple(jnp.zeros((Q_BLK, HEAD_DIM), F32) for _ in range(nh)))
    _, accs = lax.fori_loop(0, nck, body, init)
    for hh in range(nh):
        o_ref[:, hh * HEAD_DIM:(hh + 1) * HEAD_DIM] = accs[hh]


def _sb_prompt(q_all, kv_bf, ustrict2):
    rb = SEQ // Q_BLK
    nh = SB_HEADS_PER_STEP
    w = nh * HEAD_DIM
    qb_col0 = (W_A + H_I * D_IDX) // w
    kb_col0 = 2 * KV_A * HEAD_DIM // w
    vb_col0 = kb_col0 + W_B // w
    return pl.pallas_call(
        _sb_prompt_kernel,
        grid=(BATCH, H_B // nh, N_QBLK),
        in_specs=[
            pl.BlockSpec((Q_BLK, w), lambda b, h, j: (b * rb + j, qb_col0 + h)),
            pl.BlockSpec((SEQ, w), lambda b, h, j: (b, kb_col0 + h)),
            pl.BlockSpec((SEQ, w), lambda b, h, j: (b, vb_col0 + h)),
            pl.BlockSpec((2 * CK, CK), lambda b, h, j: (0, 0)),
        ],
        out_specs=pl.BlockSpec((Q_BLK, w), lambda b, h, j: (b * rb + j, h)),
        out_shape=jax.ShapeDtypeStruct((M_PROMPT, W_B), F32),
        compiler_params=_cparams(3),
        name="stick_breaking_prompt",
    )(q_all, kv_bf, kv_bf, ustrict2)


def _sel_sample_kernel(pt_ref, qi_ref, wi_ref, *rest):
    page_refs = rest[:N_PAGES]
    knew_ref, dup_ref, utri_ref, am_ref, key_ref = rest[N_PAGES:]
    s = pl.program_id(1)
    rows8 = lax.broadcasted_iota(I32, (SUBLANES, LANES), 0)
    cols8 = lax.broadcasted_iota(I32, (SUBLANES, LANES), 1)
    qi = qi_ref[...]
    wcol = wi_ref[...]
    row0 = pl.multiple_of(s * SUBLANES, SUBLANES)
    for p in range(S_TILES):
        if p < N_PAGES:
            kp = page_refs[p][...].astype(BF16)
            adm = rows8 < DEC_SEQ
        else:
            kp = knew_ref[...]
            adm = (rows8 < DEC_SEQ) & (cols8 <= rows8)
        r = jnp.maximum(_dot_nt(qi, kp), 0.0) * wcol
        sc = jnp.sum(r.reshape(SUBLANES, H_I, LANES), axis=1)
        key_ref[pl.ds(row0, SUBLANES), p * LANES:(p + 1) * LANES] = _sortable(
            jnp.where(adm, sc, -jnp.inf))

    @pl.when(s == SEQ_GROUP - 1)
    def _():
        nrow = SEQ_GROUP * SUBLANES
        rowsg = lax.broadcasted_iota(I32, (nrow, LANES), 0) % SUBLANES
        colsg = lax.broadcasted_iota(I32, (nrow, LANES), 1)
        adm_past = rowsg < DEC_SEQ
        adm_new = adm_past & (colsg <= rowsg)

        def adm_of(p):
            return adm_past if p < N_PAGES else adm_new

        def count_ge(cands):
            cbs = [jnp.broadcast_to(cand, (nrow, LANES)) for cand in cands]
            accs = [jnp.zeros((nrow, LANES), F32) for _ in cands]
            for p in range(S_TILES):
                key = key_ref[:, p * LANES:(p + 1) * LANES]
                accs = [acc + jnp.where(key >= cb, 1.0, 0.0) for acc, cb in zip(accs, cbs)]
            return [jnp.sum(acc, axis=1, keepdims=True) for acc in accs]

        pad_rows = (lax.broadcasted_iota(I32, (nrow, 1), 0) % SUBLANES) >= DEC_SEQ
        thr = _topk_threshold(count_ge, float(S_TILES * LANES), pad_rows)
        tb = jnp.broadcast_to(thr, (nrow, LANES))

        def write(p, sel):
            dup = _dot(sel.astype(BF16), dup_ref[...])
            am_ref[:, p * CK:(p + 1) * CK] = jnp.where(dup > 0.5, 0.0, NEG)

        n_sel = jnp.zeros((nrow, LANES), F32)
        for p in range(S_TILES):
            sel = jnp.where(adm_of(p),
                            jnp.where(key_ref[:, p * LANES:(p + 1) * LANES] >= tb, 1.0, 0.0), 0.0)
            n_sel = n_sel + sel
            write(p, sel)
        n_sel = jnp.sum(n_sel, axis=1, keepdims=True)

        @pl.when(jnp.max(n_sel) > TOPK)
        def _():
            n_gt = jnp.zeros((nrow, LANES), F32)
            for p in range(S_TILES):
                n_gt = n_gt + jnp.where(
                    adm_of(p),
                    jnp.where(key_ref[:, p * LANES:(p + 1) * LANES] > tb, 1.0, 0.0), 0.0)
            keep = TOPK - jnp.sum(n_gt, axis=1, keepdims=True)
            seen = jnp.zeros((nrow, 1), F32)
            for p in range(S_TILES):
                key = key_ref[:, p * LANES:(p + 1) * LANES]
                eq = jnp.where(adm_of(p), jnp.where(key == tb, 1.0, 0.0), 0.0)
                rank = seen + _dot(eq.astype(BF16), utri_ref[...])
                sel = jnp.where(adm_of(p), jnp.where(key > tb, 1.0, 0.0), 0.0)
                sel = jnp.where(eq > 0.0, jnp.where(rank <= keep, 1.0, 0.0), sel)
                write(p, sel)
                seen = seen + jnp.sum(eq, axis=1, keepdims=True)


def _sel_sample(page_table, layer, qi_s, wi_s, cache_kidx, kidx_new, dup, utri):
    def page_map(p):
        return lambda g, s, pt: (pt[g * SEQ_GROUP + s, p], layer, 0, 0)

    seq_map = lambda g, s, pt: (g * SEQ_GROUP + s, 0, 0)
    grid_spec = pltpu.PrefetchScalarGridSpec(
        num_scalar_prefetch=1,
        grid=(N_SEQ_GROUPS, SEQ_GROUP),
        in_specs=[pl.BlockSpec((None, LANES, D_IDX), seq_map),
                  pl.BlockSpec((None, LANES, 1), seq_map)]
        + [pl.BlockSpec((None, None, PAGE_SIZE, D_IDX), page_map(p)) for p in range(N_PAGES)]
        + [pl.BlockSpec((None, LANES, D_IDX), seq_map),
           pl.BlockSpec((LANES, CK), lambda g, s, pt: (0, 0)),
           pl.BlockSpec((LANES, LANES), lambda g, s, pt: (0, 0))],
        out_specs=pl.BlockSpec((None, SEQ_GROUP * SUBLANES, S_TILES * CK),
                               lambda g, s, pt: (g, 0, 0)),
        scratch_shapes=[pltpu.VMEM((SEQ_GROUP * SUBLANES, S_TILES * LANES), I32)],
    )
    return pl.pallas_call(
        _sel_sample_kernel,
        grid_spec=grid_spec,
        out_shape=jax.ShapeDtypeStruct((N_SEQ_GROUPS, SEQ_GROUP * SUBLANES, S_TILES * CK), F32),
        compiler_params=_cparams(2),
        name="indexer_select_sample",
    )(page_table, qi_s, wi_s, *([cache_kidx] * N_PAGES), kidx_new, dup, utri)


def _decode_kernel(pt_ref, qb_ref, qa_ref, kbn_ref, vbn_ref, kan_ref, van_ref, am_ref, amn_ref,
                   strip_ref, far_ref, uj2_ref, *rest):
    P = PAGES_PER_STEP
    kb_refs, vb_refs = rest[0:P], rest[P:2 * P]
    ka_refs, va_refs = rest[2 * P:3 * P], rest[3 * P:4 * P]
    ob_ref, oa_ref, accb_ref, carry_ref, m_ref, l_ref, acca_ref = rest[4 * P:]
    ci = pl.program_id(1)
    nrow = DEC_SEQ * H_B

    rows = lax.broadcasted_iota(I32, (nrow, CK), 0)
    lan = lax.broadcasted_iota(I32, (nrow, CK), 1)
    head_b = (lan % H_B) == (rows % H_B)
    group_a = (lan % KV_A) == ((rows % H_A) // G_A)
    qb = qb_ref[...]
    qa = qa_ref[...]

    def am_rows(am8):
        return jnp.concatenate(
            [jnp.broadcast_to(am8[q:q + 1, :], (H_A, am8.shape[1])) for q in range(DEC_SEQ)], axis=0)

    def dsa_update(lg, pv):
        m = m_ref[...]
        m_new = jnp.maximum(m, jnp.max(lg, axis=1, keepdims=True))
        alpha = jnp.exp(m - m_new)
        p = jnp.exp(lg - m_new)
        l_ref[...] = alpha * l_ref[...] + jnp.sum(p, axis=1, keepdims=True)
        acca_ref[...] = alpha * acca_ref[...] + pv(p.astype(BF16))
        m_ref[...] = m_new

    @pl.when(ci == 0)
    def _():
        m_ref[...] = jnp.full((nrow, 1), -jnp.inf, F32)
        l_ref[...] = jnp.zeros((nrow, 1), F32)
        acca_ref[...] = jnp.zeros((nrow, HEAD_DIM), F32)
        rows1 = lax.broadcasted_iota(I32, (nrow, LANES), 0)
        lan1 = lax.broadcasted_iota(I32, (nrow, LANES), 1)
        head_b1 = (lan1 % H_B) == (rows1 % H_B)
        group_a1 = (lan1 % KV_A) == ((rows1 % H_A) // G_A)
        vis = head_b1 & ((lan1 // H_B) < (rows1 // H_B))
        z = _dot_nt(qb, kbn_ref[...]) * ATT_SCALE
        sp = _softplus(z)
        lm = jnp.where(vis, -sp, 0.0)
        hl = _hi_lo(lm)
        u1 = jnp.concatenate([uj2_ref[:LANES, :LANES], uj2_ref[:LANES, :LANES]], axis=0)
        j1 = jnp.concatenate([uj2_ref[:LANES, CK:CK + LANES], uj2_ref[:LANES, CK:CK + LANES]], axis=0)
        a = jnp.where(vis, jnp.exp(z - sp + _dot(hl, u1)), 0.0)
        tot = _dot(hl, j1)
        accb_ref[...] = _dot(a.astype(BF16), vbn_ref[...])
        carry_ref[...] = jnp.concatenate([tot, tot], axis=1)
        lg = _dot_nt(qa, kan_ref[...]) * ATT_SCALE + strip_ref[:, CK:CK + LANES] \
            + am_rows(amn_ref[...])[:, :LANES]
        dsa_update(jnp.where(group_a1, lg, NEG), lambda p: _dot(p, van_ref[...]))

    nsub = PAGE_SIZE * H_B // CK
    blocks = []
    for i in range(P):
        z = _dot_nt(qb, kb_refs[i][...].astype(BF16)) * ATT_SCALE
        for sb in reversed(range(nsub)):
            blocks.append(z[:, sb * CK:(sb + 1) * CK])
    sps = [_softplus(zb) for zb in blocks]
    lms = [jnp.where(head_b, -sp, 0.0) for sp in sps]
    r = _dot(jnp.concatenate([_hi_lo(lm) for lm in lms], axis=0), uj2_ref[...])
    carry = carry_ref[...]
    a_blocks = []
    for k, zb in enumerate(blocks):
        within = r[k * nrow:(k + 1) * nrow, :CK]
        a_blocks.append(jnp.where(head_b, jnp.exp(zb - sps[k] + within + carry), 0.0).astype(BF16))
        carry = carry + r[k * nrow:(k + 1) * nrow, CK:]
    carry_ref[...] = carry
    accb = accb_ref[...]
    for i in range(P):
        a_page = jnp.concatenate(a_blocks[i * nsub:(i + 1) * nsub][::-1], axis=1)
        accb = accb + _dot(a_page, vb_refs[i][...].astype(BF16))
    accb_ref[...] = accb

    far = jnp.broadcast_to(far_ref[...][:, :1], (nrow, CK))
    am_all = am_rows(am_ref[...])
    lgs = []
    for i in range(P):
        lane0 = (P - 1 - i) * CK
        bias = jnp.where(ci == 0, strip_ref[:, :CK], far) if i == 0 else far
        lg = _dot_nt(qa, ka_refs[i][...].astype(BF16)) * ATT_SCALE + bias \
            + am_all[:, lane0:lane0 + CK]
        lgs.append(jnp.where(group_a, lg, NEG))

    def pv_pages(p):
        out = jnp.zeros((nrow, HEAD_DIM), F32)
        for i in range(P):
            out = out + _dot(p[:, i * CK:(i + 1) * CK], va_refs[i][...].astype(BF16))
        return out

    dsa_update(jnp.concatenate(lgs, axis=1), pv_pages)

    @pl.when(ci == N_PAGE_STEPS - 1)
    def _():
        ob_ref[...] = accb_ref[...]
        oa_ref[...] = acca_ref[...] / l_ref[...]


def _decode(page_table, layer, qb_s, qa_s, kb_new, vb_new, ka_new, va_new, am_s, strip, far,
            uj2, ckb, cvb, cka, cva):
    P = PAGES_PER_STEP

    def page_map(i, ndim):
        def f(s, ci, pt):
            return (pt[s, N_PAGES - 1 - (P * ci + i)], layer) + (0,) * (ndim - 2)
        return f

    seq_map = lambda s, ci, pt: (s, 0, 0)
    const2 = lambda s, ci, pt: (0, 0)
    nrow = DEC_SEQ * H_B
    rows_b = PAGE_SIZE * H_B
    rows_a = PAGE_SIZE * KV_A
    in_specs = [
        pl.BlockSpec((None, nrow, HEAD_DIM), seq_map),
        pl.BlockSpec((None, nrow, HEAD_DIM), seq_map),
        pl.BlockSpec((None, LANES, HEAD_DIM), seq_map),
        pl.BlockSpec((None, LANES, HEAD_DIM), seq_map),
        pl.BlockSpec((None, LANES, HEAD_DIM), seq_map),
        pl.BlockSpec((None, LANES, HEAD_DIM), seq_map),
        pl.BlockSpec((None, SUBLANES, P * CK), lambda s, ci, pt: (s, 0, N_PAGE_STEPS - 1 - ci)),
        pl.BlockSpec((None, SUBLANES, CK), lambda s, ci, pt: (s, 0, N_PAGES)),
        pl.BlockSpec((nrow, 2 * CK), const2),
        pl.BlockSpec((nrow, LANES), const2),
        pl.BlockSpec((2 * CK, 2 * CK), const2),
    ]
    in_specs += [pl.BlockSpec((None, None, rows_b, HEAD_DIM), page_map(i, 4)) for i in range(P)] * 2
    in_specs += [pl.BlockSpec((None, None, rows_a, HEAD_DIM), page_map(i, 4)) for i in range(P)] * 2
    grid_spec = pltpu.PrefetchScalarGridSpec(
        num_scalar_prefetch=1,
        grid=(DEC_BATCH, N_PAGE_STEPS),
        in_specs=in_specs,
        out_specs=[pl.BlockSpec((None, nrow, HEAD_DIM), seq_map)] * 2,
        scratch_shapes=[pltpu.VMEM((nrow, HEAD_DIM), F32), pltpu.VMEM((nrow, CK), F32),
                        pltpu.VMEM((nrow, 1), F32), pltpu.VMEM((nrow, 1), F32),
                        pltpu.VMEM((nrow, HEAD_DIM), F32)],
    )
    return pl.pallas_call(
        _decode_kernel,
        grid_spec=grid_spec,
        out_shape=[jax.ShapeDtypeStruct((DEC_BATCH, nrow, HEAD_DIM), F32)] * 2,
        compiler_params=_cparams(2),
        name="decode_attention_sample",
    )(page_table, qb_s, qa_s, kb_new, vb_new, ka_new, va_new, am_s, am_s, strip, far, uj2,
      *([ckb] * P), *([cvb] * P), *([cka] * P), *([cva] * P))


def _merge_kernel(oa_ref, ga_ref, ob_ref, gb_ref, ma_ref, mb_ref, wpa_ref, wpb_ref, u_ref):
    ga = ga_ref[...]
    gb = gb_ref[...]
    ya = _dot((oa_ref[...] * (ga * jax.nn.sigmoid(ga))).astype(BF16), wpa_ref[...])
    yb = _dot((ob_ref[...] * (gb * jax.nn.sigmoid(gb))).astype(BF16), wpb_ref[...])
    u = jax.nn.sigmoid(ma_ref[...]) * ya + jax.nn.sigmoid(mb_ref[...]) * yb
    u_ref[...] = u.astype(BF16)


def _merge(o_a, o_b, gates, w_pa, w_pb, tm=256):
    return pl.pallas_call(
        _merge_kernel,
        grid=(M_ALL // tm,),
        in_specs=[pl.BlockSpec((tm, W_A), lambda i: (i, 0)),
                  pl.BlockSpec((tm, W_A), lambda i: (i, 0)),
                  pl.BlockSpec((tm, W_B), lambda i: (i, 0)),
                  pl.BlockSpec((tm, W_B), lambda i: (i, 1)),
                  pl.BlockSpec((tm, D_MODEL), lambda i: (i, 1)),
                  pl.BlockSpec((tm, D_MODEL), lambda i: (i, 2)),
                  pl.BlockSpec((W_A, D_MODEL), lambda i: (0, 0)),
                  pl.BlockSpec((W_B, D_MODEL), lambda i: (0, 0))],
        out_specs=pl.BlockSpec((tm, D_MODEL), lambda i: (i, 0)),
        out_shape=jax.ShapeDtypeStruct((M_ALL, D_MODEL), BF16),
        compiler_params=_cparams(1),
        name="gated_merge",
    )(o_a, gates, o_b, gates, gates, gates, w_pa, w_pb)


def _outproj_kernel(u_ref, x_ref, w_ref, g_ref, b_ref, o_ref):
    mix = _dot(u_ref[...], w_ref[...])
    o_ref[...] = _layer_norm_rows(ALPHA * x_ref[...] + mix, g_ref[...], b_ref[...])


def _outproj(u, x, w_out, g, b, tm=256):
    return pl.pallas_call(
        _outproj_kernel,
        grid=(M_ALL // tm,),
        in_specs=[pl.BlockSpec((tm, D_MODEL), lambda i: (i, 0)),
                  pl.BlockSpec((tm, D_MODEL), lambda i: (i, 0)),
                  pl.BlockSpec((D_MODEL, D_MODEL), lambda i: (0, 0)),
                  pl.BlockSpec((1, D_MODEL), lambda i: (0, 0)),
                  pl.BlockSpec((1, D_MODEL), lambda i: (0, 0))],
        out_specs=pl.BlockSpec((tm, D_MODEL), lambda i: (i, 0)),
        out_shape=jax.ShapeDtypeStruct((M_ALL, D_MODEL), F32),
        compiler_params=_cparams(1),
        name="out_proj_norm",
    )(u, x, w_out, g, b)


def _ple_kernel(x_ref, p_ref, wg_ref, we_ref, g_ref, b_ref, o_ref):
    x = x_ref[...]
    gate = jax.nn.sigmoid(_dot(x.astype(BF16), wg_ref[...]))
    ple = gate * _dot(p_ref[...].astype(BF16), we_ref[...])
    o_ref[...] = _layer_norm_rows(ALPHA * x + ple, g_ref[...], b_ref[...])


def _ple(x, p, w_pg, w_pe, g, b, tm=256):
    return pl.pallas_call(
        _ple_kernel,
        grid=(M_ALL // tm,),
        in_specs=[pl.BlockSpec((tm, D_MODEL), lambda i: (i, 0)),
                  pl.BlockSpec((tm, PLE_DIM), lambda i: (i, 0)),
                  pl.BlockSpec((D_MODEL, D_MODEL), lambda i: (0, 0)),
                  pl.BlockSpec((PLE_DIM, D_MODEL), lambda i: (0, 0)),
                  pl.BlockSpec((1, D_MODEL), lambda i: (0, 0)),
                  pl.BlockSpec((1, D_MODEL), lambda i: (0, 0))],
        out_specs=pl.BlockSpec((tm, D_MODEL), lambda i: (i, 0)),
        out_shape=jax.ShapeDtypeStruct((M_ALL, D_MODEL), F32),
        compiler_params=_cparams(1),
        name="ple_update_norm",
    )(x, p, w_pg, w_pe, g, b)


def _tri_constants():
    a = jnp.arange(CK, dtype=I32)
    ustrict = (a[:, None] > a[None, :]).astype(BF16)
    ustrict2 = jnp.concatenate([ustrict, ustrict], axis=0)
    utri = (a[:, None] <= a[None, :]).astype(BF16)
    same_head = (a[:, None] % H_B) == (a[None, :] % H_B)
    u_b = (same_head & ((a[:, None] // H_B) > (a[None, :] // H_B))).astype(BF16)
    j_b = same_head.astype(BF16)
    uj = jnp.concatenate([u_b, j_b], axis=1)
    uj2 = jnp.concatenate([uj, uj], axis=0)
    s = jnp.arange(LANES, dtype=I32)
    dup = (s[:, None] == (a[None, :] // KV_A)).astype(BF16)
    return ustrict2, utri, uj2, dup


def kernel(x_prompt, x_sample, cache_k_a, cache_v_a, cache_kidx, cache_k_b, cache_v_b, page_table,
           p_prompt, p_sample, w_in, ln_kidx_g, ln_kidx_b, rel_bias, w_pa, w_pb, w_out,
           ln1_g, ln1_b, w_pe, w_pg, ln2_g, ln2_b):
    n_pool = cache_k_a.shape[0]
    ckb = cache_k_b.reshape(n_pool, DEPTH, PAGE_SIZE * H_B, HEAD_DIM)
    cvb = cache_v_b.reshape(n_pool, DEPTH, PAGE_SIZE * H_B, HEAD_DIM)
    cka = cache_k_a.reshape(n_pool, DEPTH, PAGE_SIZE * KV_A, HEAD_DIM)
    cva = cache_v_a.reshape(n_pool, DEPTH, PAGE_SIZE * KV_A, HEAD_DIM)

    ustrict2, utri, uj2, dup = _tri_constants()
    bias_tiles, strip, far = _bias_tables(rel_bias)

    o_qa, o_ka, o_va, o_ga = 0, W_A, W_A + 256, W_A + 512
    o_qi = o_ga + W_A
    o_ki = o_qi + H_I * D_IDX
    o_wi = o_ki + D_IDX
    o_qb = o_wi + H_I
    o_kb, o_vb, o_gb = o_qb + W_B, o_qb + 2 * W_B, o_qb + 3 * W_B
    o_ma = o_gb + W_B
    o_mb = o_ma + D_MODEL

    x = jnp.concatenate([x_prompt.reshape(M_PROMPT, D_MODEL), x_sample.reshape(M_SAMPLE, D_MODEL)], 0)
    p_all = jnp.concatenate([p_prompt.reshape(DEPTH, M_PROMPT, PLE_DIM),
                             p_sample.reshape(DEPTH, M_SAMPLE, PLE_DIM)], axis=1)

    rows_p = {n: [] for n in ("k_a", "v_a", "kidx", "k_b", "v_b")}
    rows_s = {n: [] for n in ("k_a", "v_a", "kidx", "k_b", "v_b")}

    for i in range(DEPTH):
        w = w_in[i]
        cols = lambda a, n: w[:, a:a + n].astype(BF16)
        w_q = jnp.concatenate([cols(o_qa, W_A), cols(o_qi, H_I * D_IDX), cols(o_qb, W_B)], axis=1)
        w_kv = jnp.concatenate([cols(o_ka, 256), cols(o_va, 256), cols(o_kb, W_B), cols(o_vb, W_B)], axis=1)
        w_g = jnp.concatenate([cols(o_ga, W_A), cols(o_gb, W_B), cols(o_ma, D_MODEL), cols(o_mb, D_MODEL)], axis=1)
        w_kw = jnp.concatenate([cols(o_ki, D_IDX), cols(o_ki, D_IDX), cols(o_wi, H_I),
                                jnp.zeros((D_MODEL, LANES - H_I), BF16)], axis=1)
        g2 = jnp.concatenate([ln_kidx_g[i], ln_kidx_g[i]])[None, :]
        b2 = jnp.concatenate([ln_kidx_b[i], ln_kidx_b[i]])[None, :]

        xb = x.astype(BF16)
        (q_all,) = _matmul(xb, w_q, [BF16])
        kv_f, kv_bf = _matmul(xb, w_kv, [F32, BF16])
        (gates,) = _matmul(xb, w_g, [F32])
        kidx2, ki2_bf, wi_all = _kiw(xb, w_kw, g2, b2)

        kvp, kvs = kv_f[:M_PROMPT], kv_f[M_PROMPT:]
        rows_p["k_a"].append(kvp[:, 0:256].reshape(BATCH, SEQ, KV_A, HEAD_DIM))
        rows_p["v_a"].append(kvp[:, 256:512].reshape(BATCH, SEQ, KV_A, HEAD_DIM))
        rows_p["k_b"].append(kvp[:, 512:512 + W_B].reshape(BATCH, SEQ, H_B, HEAD_DIM))
        rows_p["v_b"].append(kvp[:, 512 + W_B:].reshape(BATCH, SEQ, H_B, HEAD_DIM))
        rows_p["kidx"].append(kidx2[:M_PROMPT, :D_IDX].reshape(BATCH, SEQ, D_IDX))
        rows_s["k_a"].append(kvs[:, 0:256].reshape(DEC_BATCH, DEC_SEQ, KV_A, HEAD_DIM))
        rows_s["v_a"].append(kvs[:, 256:512].reshape(DEC_BATCH, DEC_SEQ, KV_A, HEAD_DIM))
        rows_s["k_b"].append(kvs[:, 512:512 + W_B].reshape(DEC_BATCH, DEC_SEQ, H_B, HEAD_DIM))
        rows_s["v_b"].append(kvs[:, 512 + W_B:].reshape(DEC_BATCH, DEC_SEQ, H_B, HEAD_DIM))
        rows_s["kidx"].append(kidx2[M_PROMPT:, :D_IDX].reshape(DEC_BATCH, DEC_SEQ, D_IDX))

        oa_p = _dsa_prompt(q_all, wi_all, ki2_bf, kv_bf, bias_tiles, utri)
        ob_p = _sb_prompt(q_all, kv_bf, ustrict2)

        qs = q_all[M_PROMPT:]
        kvs_bf = kv_bf[M_PROMPT:]
        qa_s = qs[:, :W_A].reshape(DEC_BATCH, DEC_SEQ * H_A, HEAD_DIM)
        qb_s = qs[:, W_A + H_I * D_IDX:].reshape(DEC_BATCH, DEC_SEQ * H_B, HEAD_DIM)
        qi_s = qs[:, W_A:W_A + H_I * D_IDX].reshape(DEC_BATCH, DEC_SEQ * H_I, D_IDX)
        qi_s = jnp.pad(qi_s, ((0, 0), (0, LANES - DEC_SEQ * H_I), (0, 0)))
        wi_s = wi_all[M_PROMPT:, :H_I].reshape(DEC_BATCH, DEC_SEQ * H_I, 1)
        wi_s = jnp.pad(wi_s, ((0, 0), (0, LANES - DEC_SEQ * H_I), (0, 0)))
        kidx_new = ki2_bf[M_PROMPT:, :D_IDX].reshape(DEC_BATCH, DEC_SEQ, D_IDX)
        kidx_new = jnp.pad(kidx_new, ((0, 0), (0, LANES - DEC_SEQ), (0, 0)))

        def new_tile(a, heads, slots):
            a = a.reshape(DEC_BATCH, DEC_SEQ, heads, HEAD_DIM)
            a = jnp.pad(a, ((0, 0), (0, slots - DEC_SEQ), (0, 0), (0, 0)))
            return a.reshape(DEC_BATCH, slots * heads, HEAD_DIM)

        ka_new = new_tile(kvs_bf[:, 0:256], KV_A, NEW_A)
        va_new = new_tile(kvs_bf[:, 256:512], KV_A, NEW_A)
        kb_new = new_tile(kvs_bf[:, 512:512 + W_B], H_B, NEW_B)
        vb_new = new_tile(kvs_bf[:, 512 + W_B:], H_B, NEW_B)

        am = _sel_sample(page_table, i, qi_s, wi_s, cache_kidx, kidx_new, dup, utri[:LANES, :LANES])
        am_s = am.reshape(DEC_BATCH, SUBLANES, S_TILES * CK)
        ob_s, oa_s = _decode(page_table, i, qb_s, qa_s, kb_new, vb_new, ka_new, va_new, am_s,
                             strip, far, uj2, ckb, cvb, cka, cva)

        o_a = jnp.concatenate([oa_p, oa_s.reshape(M_SAMPLE, W_A)], axis=0)
        o_b = jnp.concatenate([ob_p, ob_s.reshape(M_SAMPLE, W_B)], axis=0)

        u = _merge(o_a, o_b, gates, w_pa[i].astype(BF16), w_pb[i].astype(BF16))
        x = _outproj(u, x, w_out[i].astype(BF16), ln1_g[i][None, :], ln1_b[i][None, :])
        x = _ple(x, p_all[i], w_pg[i].astype(BF16), w_pe[i].astype(BF16),
                 ln2_g[i][None, :], ln2_b[i][None, :])

    st = lambda rows: jnp.stack(rows, axis=1)
    return (x[:M_PROMPT].reshape(BATCH, SEQ, D_MODEL), x[M_PROMPT:].reshape(DEC_BATCH, DEC_SEQ, D_MODEL),
            st(rows_p["k_a"]), st(rows_p["v_a"]), st(rows_p["kidx"]), st(rows_p["k_b"]), st(rows_p["v_b"]),
            st(rows_s["k_a"]), st(rows_s["v_a"]), st(rows_s["kidx"]), st(rows_s["k_b"]), st(rows_s["v_b"]))
```

```python
import functools
import math

import jax
import jax.numpy as jnp
from jax import lax
from jax.experimental import pallas as pl
from jax.experimental.pallas import tpu as pltpu

F32 = jnp.float32
BF16 = jnp.bfloat16
I32 = jnp.int32

D_MODEL = 2048
BATCH = 4
SEQ = 2048
DEPTH = 2
DEC_BATCH = 128
DEC_SEQ = 4
PAST_LEN = 2048
PAGE_SIZE = 128
N_PAGES = PAST_LEN // PAGE_SIZE
HEAD_DIM = 128
H_A = 8
KV_A = 2
G_A = H_A // KV_A
W_A = H_A * HEAD_DIM
H_I = 16
D_IDX = 64
TOPK = 256
H_B = 8
W_B = H_B * HEAD_DIM
N_BUCKETS = 32
MAX_DISTANCE = 128
PLE_DIM = 256
LN_EPS = 1e-5
ALPHA = (2 * DEPTH) ** 0.25

M_PROMPT = BATCH * SEQ
M_SAMPLE = DEC_BATCH * DEC_SEQ
M_ALL = M_PROMPT + M_SAMPLE

LANES = 128
SUBLANES = 8
Q_BLK = 128
CK = 256
N_QBLK = SEQ // Q_BLK
NEG = -1e30
INT_MIN = -2 ** 31
ATT_SCALE = HEAD_DIM ** -0.5
VMEM_LIMIT = 52 * 1024 * 1024

SEQ_GROUP = 32
N_SEQ_GROUPS = DEC_BATCH // SEQ_GROUP
S_TILES = N_PAGES + 1
PAGES_PER_STEP = 8
N_PAGE_STEPS = N_PAGES // PAGES_PER_STEP
PROJ_TM = M_ALL // 4
PROJ_TN = 512
SB_HEADS_PER_STEP = 4
SEARCH_BITS_PER_CHECK = 4
NEW_B = LANES // H_B
NEW_A = LANES // KV_A


def _cparams(n_axes):
    return pltpu.CompilerParams(dimension_semantics=("arbitrary",) * n_axes,
                                vmem_limit_bytes=VMEM_LIMIT)


def _dot_nt(a, b):
    return lax.dot_general(a, b, (((1,), (1,)), ((), ())), preferred_element_type=F32)


def _dot(a, b):
    return jnp.dot(a, b, preferred_element_type=F32)


def _sortable(s):
    b = pltpu.bitcast(s, I32)
    return jnp.where(b < 0, b ^ jnp.int32(0x7FFFFFFF), b)


def _hi_lo(x):
    hi = x.astype(BF16)
    lo = (x - hi.astype(F32)).astype(BF16)
    return jnp.concatenate([hi, lo], axis=1)


def _softplus(z):
    return jnp.maximum(z, 0.0) + jnp.log(1.0 + jnp.exp(-jnp.abs(z)))


def _layer_norm_rows(x, g, b):
    mu = jnp.mean(x, axis=-1, keepdims=True)
    xc = x - mu
    var = jnp.mean(xc * xc, axis=-1, keepdims=True)
    return xc * lax.rsqrt(var + LN_EPS) * g + b


def _mm_kernel(x_ref, w_ref, *o_refs):
    acc = _dot(x_ref[...], w_ref[...])
    for o in o_refs:
        o[...] = acc.astype(o.dtype)


def _proj(x, w, layer, col0, n, out_dtypes):
    m, k = x.shape
    tm, tn = PROJ_TM, PROJ_TN
    cb0 = col0 // tn
    return pl.pallas_call(
        _mm_kernel,
        grid=(m // tm, n // tn),
        in_specs=[pl.BlockSpec((tm, k), lambda i, j: (i, 0)),
                  pl.BlockSpec((None, k, tn), lambda i, j: (layer, 0, cb0 + j))],
        out_specs=[pl.BlockSpec((tm, tn), lambda i, j: (i, j)) for _ in out_dtypes],
        out_shape=[jax.ShapeDtypeStruct((m, n), d) for d in out_dtypes],
        compiler_params=_cparams(2),
        name="proj_matmul",
    )(x, w)


def _kiw_kernel(x_ref, w_ref, g_ref, b_ref, kf_ref, kb_ref, wi_ref):
    h = _dot(x_ref[...], w_ref[...])
    lane = lax.broadcasted_iota(I32, h.shape, 1)
    is_k = lane < D_IDX
    mu = jnp.sum(jnp.where(is_k, h, 0.0), axis=-1, keepdims=True) * (1.0 / D_IDX)
    hc = jnp.where(is_k, h - mu, 0.0)
    var = jnp.sum(hc * hc, axis=-1, keepdims=True) * (1.0 / D_IDX)
    kn = hc * lax.rsqrt(var + LN_EPS)
    k2 = jnp.where(is_k, kn, pltpu.roll(kn, D_IDX, 1)) * g_ref[...] + b_ref[...]
    kf_ref[...] = k2
    kb_ref[...] = k2.astype(BF16)
    wi_ref[...] = h * (H_I ** -0.5 * D_IDX ** -0.5)


def _kiw(x, w, layer, col0, g2, b2, tm=512):
    m, k = x.shape
    return pl.pallas_call(
        _kiw_kernel,
        grid=(m // tm,),
        in_specs=[pl.BlockSpec((tm, k), lambda i: (i, 0)),
                  pl.BlockSpec((None, k, LANES), lambda i: (layer, 0, col0 // LANES)),
                  pl.BlockSpec((1, LANES), lambda i: (0, 0)),
                  pl.BlockSpec((1, LANES), lambda i: (0, 0))],
        out_specs=[pl.BlockSpec((tm, LANES), lambda i: (i, 0))] * 3,
        out_shape=[jax.ShapeDtypeStruct((m, LANES), F32),
                   jax.ShapeDtypeStruct((m, LANES), BF16),
                   jax.ShapeDtypeStruct((m, LANES), F32)],
        compiler_params=_cparams(1),
        name="indexer_key_proj",
    )(x, w, g2, b2)


def _t5_bucket(d):
    n = jnp.maximum(d, 0)
    max_exact = N_BUCKETS // 2
    nf = jnp.maximum(n, 1).astype(F32)
    large = max_exact + (jnp.log(nf / max_exact) / math.log(MAX_DISTANCE / max_exact)
                         * (N_BUCKETS - max_exact)).astype(I32)
    return jnp.where(n < max_exact, n, jnp.minimum(large, N_BUCKETS - 1))


def _bias_kernel(rb_ref, tiles_ref, strip_ref, far_ref):
    ii = lax.broadcasted_iota(I32, (Q_BLK, LANES), 0)
    jj = lax.broadcasted_iota(I32, (Q_BLK, LANES), 1)
    bk0 = _t5_bucket(ii - jj)
    bk1 = _t5_bucket(ii - jj + LANES)
    for n in range(H_A):
        t0 = jnp.zeros((Q_BLK, LANES), F32)
        t1 = jnp.zeros((Q_BLK, LANES), F32)
        for b in range(N_BUCKETS):
            t0 = jnp.where(bk0 == b, rb_ref[b, n], t0)
            t1 = jnp.where(bk1 == b, rb_ref[b, n], t1)
        tiles_ref[n, 0] = t0
        tiles_ref[n, 1] = t1
        tiles_ref[n, 2] = jnp.full((Q_BLK, LANES), rb_ref[N_BUCKETS - 1, n], F32)
    rows = lax.broadcasted_iota(I32, (DEC_SEQ * H_A, 2 * CK), 0)
    lan = lax.broadcasted_iota(I32, (DEC_SEQ * H_A, 2 * CK), 1)
    q = rows // H_A
    n_of_row = rows % H_A
    pos = jnp.where(lan < CK, PAST_LEN - PAGE_SIZE + lan // KV_A, PAST_LEN + (lan - CK) // KV_A)
    bks = _t5_bucket(PAST_LEN + q - pos)
    strip = jnp.zeros((DEC_SEQ * H_A, 2 * CK), F32)
    far = jnp.zeros((DEC_SEQ * H_A, LANES), F32)
    rows_f = lax.broadcasted_iota(I32, (DEC_SEQ * H_A, LANES), 0) % H_A
    for n in range(H_A):
        sn = jnp.zeros((DEC_SEQ * H_A, 2 * CK), F32)
        for b in range(N_BUCKETS):
            sn = jnp.where(bks == b, rb_ref[b, n], sn)
        strip = jnp.where(n_of_row == n, sn, strip)
        far = jnp.where(rows_f == n, rb_ref[N_BUCKETS - 1, n], far)
    strip_ref[...] = strip
    far_ref[...] = far


def _bias_tables(rel_bias):
    return pl.pallas_call(
        _bias_kernel,
        in_specs=[pl.BlockSpec(memory_space=pltpu.SMEM)],
        out_shape=[jax.ShapeDtypeStruct((H_A, 3, Q_BLK, LANES), F32),
                   jax.ShapeDtypeStruct((DEC_SEQ * H_A, 2 * CK), F32),
                   jax.ShapeDtypeStruct((DEC_SEQ * H_A, LANES), F32)],
        name="t5_bias_tables",
    )(rel_bias)


def _topk_threshold(count_ge, n_keys, few):
    rows = few.shape[0]
    t_init = jnp.full((rows, 1), INT_MIN, I32)
    cnt_init = jnp.zeros((rows, 1), F32) + n_keys
    passes_per_check = SEARCH_BITS_PER_CHECK // 2

    def pending(cnt):
        return (jnp.max(jnp.where(few, 0.0, jnp.abs(cnt - TOPK))) > 0.0).astype(I32)

    def steps(state):
        g, t, cnt, _ = state
        for b in range(passes_per_check):
            i = g * passes_per_check + b
            hi = jnp.left_shift(jnp.int32(1), 31 - 2 * i)
            lo = jnp.left_shift(jnp.int32(1), 30 - 2 * i)
            cands = [t + lo, t + hi, t + hi + lo]
            counts = count_ge(cands)
            for cand, c in zip(cands, counts):
                ok = c >= TOPK
                t = jnp.where(ok, cand, t)
                cnt = jnp.where(ok, c, cnt)
        return g + 1, t, cnt, pending(cnt)

    def cond(state):
        g, _, _, flag = state
        return (g < 32 // SEARCH_BITS_PER_CHECK) & (flag > 0)

    _, t, _, _ = lax.while_loop(cond, steps, (jnp.int32(0), t_init, cnt_init, pending(cnt_init)))
    return jnp.where(few, jnp.int32(INT_MIN), t)


def _dsa_prompt_kernel(qa_ref, qi_ref, wi_ref, ki2_ref, ka_ref, va_ref, bias_ref, utri_ref,
                       o_ref, qim_ref, qg_ref, key_ref, am_ref, m_ref, acc_ref):
    j = pl.program_id(1)
    nck = j // 2 + 1
    t0 = j * Q_BLK

    lane = lax.broadcasted_iota(I32, (Q_BLK, LANES), 1)
    for pr in range(H_I // 2):
        qp = qi_ref[:, pr * LANES:(pr + 1) * LANES].astype(F32)
        qim_ref[(2 * pr) * Q_BLK:(2 * pr + 1) * Q_BLK, :] = jnp.where(lane < D_IDX, qp, 0.0).astype(BF16)
        qim_ref[(2 * pr + 1) * Q_BLK:(2 * pr + 2) * Q_BLK, :] = jnp.where(lane >= D_IDX, qp, 0.0).astype(BF16)
    for n in range(H_A):
        g, nl = divmod(n, G_A)
        qg_ref[g, nl * Q_BLK:(nl + 1) * Q_BLK, :] = qa_ref[:, n * HEAD_DIM:(n + 1) * HEAD_DIM]

    rows = t0 + lax.broadcasted_iota(I32, (Q_BLK, CK), 0)
    cols0 = lax.broadcasted_iota(I32, (Q_BLK, CK), 1)
    wi = wi_ref[...]

    def score_chunk(c, carry):
        kc = ki2_ref[pl.ds(pl.multiple_of(c * CK, CK), CK), :]
        d = _dot_nt(qim_ref[...], kc)
        s = jnp.zeros((Q_BLK, CK), F32)
        for h in range(H_I):
            s = s + wi[:, D_IDX + h:D_IDX + h + 1] * jnp.maximum(d[h * Q_BLK:(h + 1) * Q_BLK], 0.0)
        adm = (cols0 + c * CK) <= rows
        key_ref[c] = _sortable(jnp.where(adm, s, -jnp.inf))
        return carry

    lax.fori_loop(0, nck, score_chunk, 0)

    def count_ge(cands):
        cbs = [jnp.broadcast_to(cand, (Q_BLK, LANES)) for cand in cands]

        def body(c, accs):
            k0 = key_ref[c, :, :LANES]
            k1 = key_ref[c, :, LANES:]
            return tuple(acc + jnp.where(k0 >= cb, 1.0, 0.0) + jnp.where(k1 >= cb, 1.0, 0.0)
                         for acc, cb in zip(accs, cbs))

        accs = lax.fori_loop(0, nck, body, tuple(jnp.zeros((Q_BLK, LANES), F32) for _ in cands))
        return [jnp.sum(acc, axis=1, keepdims=True) for acc in accs]

    few = (t0 + lax.broadcasted_iota(I32, (Q_BLK, 1), 0)) < TOPK
    thr = _topk_threshold(count_ge, (nck * CK).astype(F32), few)
    tb = jnp.broadcast_to(thr, (Q_BLK, CK))

    def mask_chunk(c, acc):
        adm = (cols0 + c * CK) <= rows
        sel = jnp.where(adm, jnp.where(key_ref[c] >= tb, 1.0, 0.0), 0.0)
        am_ref[c] = jnp.where(sel > 0.0, 0.0, NEG)
        return acc + sel

    n_sel = jnp.sum(lax.fori_loop(0, nck, mask_chunk, jnp.zeros((Q_BLK, CK), F32)),
                    axis=1, keepdims=True)

    @pl.when(jnp.max(n_sel) > TOPK)
    def _():
        def gt_chunk(c, acc):
            adm = (cols0 + c * CK) <= rows
            return acc + jnp.where(adm, jnp.where(key_ref[c] > tb, 1.0, 0.0), 0.0)

        n_gt = jnp.sum(lax.fori_loop(0, nck, gt_chunk, jnp.zeros((Q_BLK, CK), F32)),
                       axis=1, keepdims=True)
        keep = TOPK - n_gt

        def tie_chunk(c, seen):
            adm = (cols0 + c * CK) <= rows
            key = key_ref[c]
            eq = jnp.where(adm, jnp.where(key == tb, 1.0, 0.0), 0.0)
            rank = seen + _dot(eq.astype(BF16), utri_ref[...])
            sel = jnp.where(adm, jnp.where(key > tb, 1.0, 0.0), 0.0)
            sel = jnp.where(eq > 0.0, jnp.where(rank <= keep, 1.0, 0.0), sel)
            am_ref[c] = jnp.where(sel > 0.0, 0.0, NEG)
            return seen + jnp.sum(eq, axis=1, keepdims=True)

        lax.fori_loop(0, nck, tie_chunk, jnp.zeros((Q_BLK, 1), F32))

    grows = G_A * Q_BLK

    def chunk_logits(c, g):
        start = pl.multiple_of(c * CK, CK)
        d0 = jnp.clip(j - 2 * c, 0, 2)
        d1 = jnp.clip(j - 2 * c - 1, 0, 2)
        am4 = jnp.concatenate([am_ref[c]] * G_A, axis=0)
        kc = ka_ref[pl.ds(start, CK), g * HEAD_DIM:(g + 1) * HEAD_DIM]
        bias = jnp.concatenate(
            [jnp.concatenate([bias_ref[g * G_A + nl, d0], bias_ref[g * G_A + nl, d1]], axis=1)
             for nl in range(G_A)], axis=0)
        return _dot_nt(qg_ref[g], kc) * ATT_SCALE + bias + am4

    def max_chunk(c, ms):
        out = []
        for g in range(KV_A):
            lg = chunk_logits(c, g)
            out.append(jnp.maximum(ms[g], jnp.maximum(lg[:, :LANES], lg[:, LANES:])))
        return tuple(out)

    ms = lax.fori_loop(0, nck, max_chunk,
                       tuple(jnp.full((grows, LANES), -jnp.inf, F32) for _ in range(KV_A)))
    for g in range(KV_A):
        m_ref[g] = jnp.broadcast_to(jnp.max(ms[g], axis=1, keepdims=True), (grows, LANES))
    acc_ref[...] = jnp.zeros((KV_A, grows, 2 * HEAD_DIM), F32)
    ones = jnp.ones((CK, HEAD_DIM), BF16)

    def acc_chunk(c, carry):
        start = pl.multiple_of(c * CK, CK)
        for g in range(KV_A):
            m = m_ref[g]
            p = jnp.exp(chunk_logits(c, g) - jnp.concatenate([m, m], axis=1))
            v1 = jnp.concatenate([va_ref[pl.ds(start, CK), g * HEAD_DIM:(g + 1) * HEAD_DIM], ones], axis=1)
            acc_ref[g] = acc_ref[g] + _dot(p.astype(BF16), v1)
        return carry

    lax.fori_loop(0, nck, acc_chunk, 0)
    for n in range(H_A):
        g, nl = divmod(n, G_A)
        acc = acc_ref[g, nl * Q_BLK:(nl + 1) * Q_BLK, :]
        o_ref[:, n * HEAD_DIM:(n + 1) * HEAD_DIM] = acc[:, :HEAD_DIM] / acc[:, HEAD_DIM:]


def _dsa_prompt(qa, qi, wi_all, ki2, kva_bf, bias_tiles, utri):
    rb = SEQ // Q_BLK
    grows = G_A * Q_BLK
    return pl.pallas_call(
        _dsa_prompt_kernel,
        grid=(BATCH, N_QBLK),
        in_specs=[
            pl.BlockSpec((Q_BLK, W_A), lambda b, j: (b * rb + j, 0)),
            pl.BlockSpec((Q_BLK, H_I * D_IDX), lambda b, j: (b * rb + j, 0)),
            pl.BlockSpec((Q_BLK, LANES), lambda b, j: (b * rb + j, 0)),
            pl.BlockSpec((SEQ, LANES), lambda b, j: (b, 0)),
            pl.BlockSpec((SEQ, KV_A * HEAD_DIM), lambda b, j: (b, 0)),
            pl.BlockSpec((SEQ, KV_A * HEAD_DIM), lambda b, j: (b, 1)),
            pl.BlockSpec((H_A, 3, Q_BLK, LANES), lambda b, j: (0, 0, 0, 0)),
            pl.BlockSpec((CK, CK), lambda b, j: (0, 0)),
        ],
        out_specs=pl.BlockSpec((Q_BLK, W_A), lambda b, j: (b * rb + j, 0)),
        out_shape=jax.ShapeDtypeStruct((M_ALL, W_A), F32),
        scratch_shapes=[pltpu.VMEM((H_I * Q_BLK, LANES), BF16),
                        pltpu.VMEM((KV_A, grows, HEAD_DIM), BF16),
                        pltpu.VMEM((SEQ // CK, Q_BLK, CK), I32),
                        pltpu.VMEM((SEQ // CK, Q_BLK, CK), F32),
                        pltpu.VMEM((KV_A, grows, LANES), F32),
                        pltpu.VMEM((KV_A, grows, 2 * HEAD_DIM), F32)],
        compiler_params=_cparams(2),
        name="dsa_prompt",
    )(qa, qi, wi_all, ki2, kva_bf, kva_bf, bias_tiles, utri)


def _sb_prompt_kernel(q_ref, k_ref, v_ref, u2_ref, o_ref):
    nh = SB_HEADS_PER_STEP
    j = pl.program_id(2)
    nck = j // 2 + 1
    rows = j * Q_BLK + lax.broadcasted_iota(I32, (Q_BLK, CK), 0)
    cols0 = lax.broadcasted_iota(I32, (Q_BLK, CK), 1)

    def body(i, carry):
        tots, accs = carry
        c = nck - 1 - i
        start = pl.multiple_of(c * CK, CK)
        vis = (cols0 + c * CK) < rows
        zs, sps, lms = [], [], []
        for hh in range(nh):
            hs = slice(hh * HEAD_DIM, (hh + 1) * HEAD_DIM)
            z = _dot_nt(q_ref[:, hs], k_ref[pl.ds(start, CK), hs]) * ATT_SCALE
            sp = _softplus(z)
            zs.append(z)
            sps.append(sp)
            lms.append(jnp.where(vis, -sp, 0.0))
        within = _dot(jnp.concatenate([_hi_lo(lm) for lm in lms], axis=0), u2_ref[...])
        new_tots, new_accs = [], []
        for hh in range(nh):
            hs = slice(hh * HEAD_DIM, (hh + 1) * HEAD_DIM)
            suf = within[hh * Q_BLK:(hh + 1) * Q_BLK] + tots[hh]
            a = jnp.where(vis, jnp.exp(zs[hh] - sps[hh] + suf), 0.0)
            new_accs.append(accs[hh] + _dot(a.astype(BF16), v_ref[pl.ds(start, CK), hs]))
            new_tots.append(tots[hh] + jnp.sum(lms[hh], axis=1, keepdims=True))
        return tuple(new_tots), tuple(new_accs)

    init = (tuple(jnp.zeros((Q_BLK, 1), F32) for _ in range(nh)),
            tuple(jnp.zeros((Q_BLK, HEAD_DIM), F32) for _ in range(nh)))
    _, accs = lax.fori_loop(0, nck, body, init)
    for hh in range(nh):
        o_ref[:, hh * HEAD_DIM:(hh + 1) * HEAD_DIM] = accs[hh]


def _sb_prompt(qb, kvb_bf, ustrict2):
    rb = SEQ // Q_BLK
    nh = SB_HEADS_PER_STEP
    w = nh * HEAD_DIM
    vb_col0 = W_B // w
    return pl.pallas_call(
        _sb_prompt_kernel,
        grid=(BATCH, H_B // nh, N_QBLK),
        in_specs=[
            pl.BlockSpec((Q_BLK, w), lambda b, h, j: (b * rb + j, h)),
            pl.BlockSpec((SEQ, w), lambda b, h, j: (b, h)),
            pl.BlockSpec((SEQ, w), lambda b, h, j: (b, vb_col0 + h)),
            pl.BlockSpec((2 * CK, CK), lambda b, h, j: (0, 0)),
        ],
        out_specs=pl.BlockSpec((Q_BLK, w), lambda b, h, j: (b * rb + j, h)),
        out_shape=jax.ShapeDtypeStruct((M_ALL, W_B), F32),
        compiler_params=_cparams(3),
        name="stick_breaking_prompt",
    )(qb, kvb_bf, kvb_bf, ustrict2)


def _sel_sample_kernel(pt_ref, qi_ref, wi_ref, *rest):
    page_refs = rest[:N_PAGES]
    knew_ref, dup_ref, utri_ref, am_ref, key_ref = rest[N_PAGES:]
    s = pl.program_id(1)
    rows8 = lax.broadcasted_iota(I32, (SUBLANES, LANES), 0)
    cols8 = lax.broadcasted_iota(I32, (SUBLANES, LANES), 1)
    qi = qi_ref[...]
    wcol = wi_ref[...]
    row0 = pl.multiple_of(s * SUBLANES, SUBLANES)
    for p in range(S_TILES):
        if p < N_PAGES:
            kp = page_refs[p][...].astype(BF16)
            adm = rows8 < DEC_SEQ
        else:
            kp = knew_ref[...]
            adm = (rows8 < DEC_SEQ) & (cols8 <= rows8)
        r = jnp.maximum(_dot_nt(qi, kp), 0.0) * wcol
        sc = jnp.sum(r.reshape(SUBLANES, H_I, LANES), axis=1)
        key_ref[pl.ds(row0, SUBLANES), p * LANES:(p + 1) * LANES] = _sortable(
            jnp.where(adm, sc, -jnp.inf))

    @pl.when(s == SEQ_GROUP - 1)
    def _():
        nrow = SEQ_GROUP * SUBLANES
        rowsg = lax.broadcasted_iota(I32, (nrow, LANES), 0) % SUBLANES
        colsg = lax.broadcasted_iota(I32, (nrow, LANES), 1)
        adm_past = rowsg < DEC_SEQ
        adm_new = adm_past & (colsg <= rowsg)

        def adm_of(p):
            return adm_past if p < N_PAGES else adm_new

        def count_ge(cands):
            cbs = [jnp.broadcast_to(cand, (nrow, LANES)) for cand in cands]
            accs = [jnp.zeros((nrow, LANES), F32) for _ in cands]
            for p in range(S_TILES):
                key = key_ref[:, p * LANES:(p + 1) * LANES]
                accs = [acc + jnp.where(key >= cb, 1.0, 0.0) for acc, cb in zip(accs, cbs)]
            return [jnp.sum(acc, axis=1, keepdims=True) for acc in accs]

        pad_rows = (lax.broadcasted_iota(I32, (nrow, 1), 0) % SUBLANES) >= DEC_SEQ
        thr = _topk_threshold(count_ge, float(S_TILES * LANES), pad_rows)
        tb = jnp.broadcast_to(thr, (nrow, LANES))

        def write(p, sel):
            dup = _dot(sel.astype(BF16), dup_ref[...])
            am_ref[:, p * CK:(p + 1) * CK] = jnp.where(dup > 0.5, 0.0, NEG)

        n_sel = jnp.zeros((nrow, LANES), F32)
        for p in range(S_TILES):
            sel = jnp.where(adm_of(p),
                            jnp.where(key_ref[:, p * LANES:(p + 1) * LANES] >= tb, 1.0, 0.0), 0.0)
            n_sel = n_sel + sel
            write(p, sel)
        n_sel = jnp.sum(n_sel, axis=1, keepdims=True)

        @pl.when(jnp.max(n_sel) > TOPK)
        def _():
            n_gt = jnp.zeros((nrow, LANES), F32)
            for p in range(S_TILES):
                n_gt = n_gt + jnp.where(
                    adm_of(p),
                    jnp.where(key_ref[:, p * LANES:(p + 1) * LANES] > tb, 1.0, 0.0), 0.0)
            keep = TOPK - jnp.sum(n_gt, axis=1, keepdims=True)
            seen = jnp.zeros((nrow, 1), F32)
            for p in range(S_TILES):
                key = key_ref[:, p * LANES:(p + 1) * LANES]
                eq = jnp.where(adm_of(p), jnp.where(key == tb, 1.0, 0.0), 0.0)
                rank = seen + _dot(eq.astype(BF16), utri_ref[...])
                sel = jnp.where(adm_of(p), jnp.where(key > tb, 1.0, 0.0), 0.0)
                sel = jnp.where(eq > 0.0, jnp.where(rank <= keep, 1.0, 0.0), sel)
                write(p, sel)
                seen = seen + jnp.sum(eq, axis=1, keepdims=True)


def _sel_sample(page_table, layer, qi_s, wi_s, cache_kidx, kidx_new, dup, utri):
    def page_map(p):
        return lambda g, s, pt: (pt[g * SEQ_GROUP + s, p], layer, 0, 0)

    seq_map = lambda g, s, pt: (g * SEQ_GROUP + s, 0, 0)
    grid_spec = pltpu.PrefetchScalarGridSpec(
        num_scalar_prefetch=1,
        grid=(N_SEQ_GROUPS, SEQ_GROUP),
        in_specs=[pl.BlockSpec((None, LANES, D_IDX), seq_map),
                  pl.BlockSpec((None, LANES, 1), seq_map)]
        + [pl.BlockSpec((None, None, PAGE_SIZE, D_IDX), page_map(p)) for p in range(N_PAGES)]
        + [pl.BlockSpec((None, LANES, D_IDX), seq_map),
           pl.BlockSpec((LANES, CK), lambda g, s, pt: (0, 0)),
           pl.BlockSpec((LANES, LANES), lambda g, s, pt: (0, 0))],
        out_specs=pl.BlockSpec((None, SEQ_GROUP * SUBLANES, S_TILES * CK),
                               lambda g, s, pt: (g, 0, 0)),
        scratch_shapes=[pltpu.VMEM((SEQ_GROUP * SUBLANES, S_TILES * LANES), I32)],
    )
    return pl.pallas_call(
        _sel_sample_kernel,
        grid_spec=grid_spec,
        out_shape=jax.ShapeDtypeStruct((N_SEQ_GROUPS, SEQ_GROUP * SUBLANES, S_TILES * CK), F32),
        compiler_params=_cparams(2),
        name="indexer_select_sample",
    )(page_table, qi_s, wi_s, *([cache_kidx] * N_PAGES), kidx_new, dup, utri)


def _decode_kernel(pt_ref, qb_ref, qa_ref, kbn_ref, vbn_ref, kan_ref, van_ref, am_ref, amn_ref,
                   strip_ref, far_ref, uj2_ref, *rest):
    P = PAGES_PER_STEP
    kb_refs, vb_refs = rest[0:P], rest[P:2 * P]
    ka_refs, va_refs = rest[2 * P:3 * P], rest[3 * P:4 * P]
    ob_ref, oa_ref, accb_ref, carry_ref, m_ref, l_ref, acca_ref = rest[4 * P:]
    ci = pl.program_id(1)
    nrow = DEC_SEQ * H_B

    rows = lax.broadcasted_iota(I32, (nrow, CK), 0)
    lan = lax.broadcasted_iota(I32, (nrow, CK), 1)
    head_b = (lan % H_B) == (rows % H_B)
    group_a = (lan % KV_A) == ((rows % H_A) // G_A)
    qb = qb_ref[...]
    qa = qa_ref[...]

    def am_rows(am8):
        return jnp.concatenate(
            [jnp.broadcast_to(am8[q:q + 1, :], (H_A, am8.shape[1])) for q in range(DEC_SEQ)], axis=0)

    def dsa_update(lg, pv):
        m = m_ref[...]
        m_new = jnp.maximum(m, jnp.max(lg, axis=1, keepdims=True))
        alpha = jnp.exp(m - m_new)
        p = jnp.exp(lg - m_new)
        l_ref[...] = alpha * l_ref[...] + jnp.sum(p, axis=1, keepdims=True)
        acca_ref[...] = alpha * acca_ref[...] + pv(p.astype(BF16))
        m_ref[...] = m_new

    @pl.when(ci == 0)
    def _():
        m_ref[...] = jnp.full((nrow, 1), -jnp.inf, F32)
        l_ref[...] = jnp.zeros((nrow, 1), F32)
        acca_ref[...] = jnp.zeros((nrow, HEAD_DIM), F32)
        rows1 = lax.broadcasted_iota(I32, (nrow, LANES), 0)
        lan1 = lax.broadcasted_iota(I32, (nrow, LANES), 1)
        head_b1 = (lan1 % H_B) == (rows1 % H_B)
        group_a1 = (lan1 % KV_A) == ((rows1 % H_A) // G_A)
        vis = head_b1 & ((lan1 // H_B) < (rows1 // H_B))
        z = _dot_nt(qb, kbn_ref[...]) * ATT_SCALE
        sp = _softplus(z)
        lm = jnp.where(vis, -sp, 0.0)
        hl = _hi_lo(lm)
        u1 = jnp.concatenate([uj2_ref[:LANES, :LANES], uj2_ref[:LANES, :LANES]], axis=0)
        j1 = jnp.concatenate([uj2_ref[:LANES, CK:CK + LANES], uj2_ref[:LANES, CK:CK + LANES]], axis=0)
        a = jnp.where(vis, jnp.exp(z - sp + _dot(hl, u1)), 0.0)
        tot = _dot(hl, j1)
        accb_ref[...] = _dot(a.astype(BF16), vbn_ref[...])
        carry_ref[...] = jnp.concatenate([tot, tot], axis=1)
        lg = _dot_nt(qa, kan_ref[...]) * ATT_SCALE + strip_ref[:, CK:CK + LANES] \
            + am_rows(amn_ref[...])[:, :LANES]
        dsa_update(jnp.where(group_a1, lg, NEG), lambda p: _dot(p, van_ref[...]))

    nsub = PAGE_SIZE * H_B // CK
    blocks = []
    for i in range(P):
        z = _dot_nt(qb, kb_refs[i][...].astype(BF16)) * ATT_SCALE
        for sb in reversed(range(nsub)):
            blocks.append(z[:, sb * CK:(sb + 1) * CK])
    sps = [_softplus(zb) for zb in blocks]
    lms = [jnp.where(head_b, -sp, 0.0) for sp in sps]
    r = _dot(jnp.concatenate([_hi_lo(lm) for lm in lms], axis=0), uj2_ref[...])
    carry = carry_ref[...]
    a_blocks = []
    for k, zb in enumerate(blocks):
        within = r[k * nrow:(k + 1) * nrow, :CK]
        a_blocks.append(jnp.where(head_b, jnp.exp(zb - sps[k] + within + carry), 0.0).astype(BF16))
        carry = carry + r[k * nrow:(k + 1) * nrow, CK:]
    carry_ref[...] = carry
    accb = accb_ref[...]
    for i in range(P):
        a_page = jnp.concatenate(a_blocks[i * nsub:(i + 1) * nsub][::-1], axis=1)
        accb = accb + _dot(a_page, vb_refs[i][...].astype(BF16))
    accb_ref[...] = accb

    far = jnp.broadcast_to(far_ref[...][:, :1], (nrow, CK))
    am_all = am_rows(am_ref[...])
    lgs = []
    for i in range(P):
        lane0 = (P - 1 - i) * CK
        bias = jnp.where(ci == 0, strip_ref[:, :CK], far) if i == 0 else far
        lg = _dot_nt(qa, ka_refs[i][...].astype(BF16)) * ATT_SCALE + bias \
            + am_all[:, lane0:lane0 + CK]
        lgs.append(jnp.where(group_a, lg, NEG))

    def pv_pages(p):
        out = jnp.zeros((nrow, HEAD_DIM), F32)
        for i in range(P):
            out = out + _dot(p[:, i * CK:(i + 1) * CK], va_refs[i][...].astype(BF16))
        return out

    dsa_update(jnp.concatenate(lgs, axis=1), pv_pages)

    @pl.when(ci == N_PAGE_STEPS - 1)
    def _():
        ob_ref[...] = accb_ref[...]
        oa_ref[...] = acca_ref[...] / l_ref[...]


def _decode(page_table, layer, qb_s, qa_s, kb_new, vb_new, ka_new, va_new, am_s, strip, far,
            uj2, ckb, cvb, cka, cva):
    P = PAGES_PER_STEP

    def page_map(i, ndim):
        def f(s, ci, pt):
            return (pt[s, N_PAGES - 1 - (P * ci + i)], layer) + (0,) * (ndim - 2)
        return f

    seq_map = lambda s, ci, pt: (s, 0, 0)
    const2 = lambda s, ci, pt: (0, 0)
    nrow = DEC_SEQ * H_B
    rows_b = PAGE_SIZE * H_B
    rows_a = PAGE_SIZE * KV_A
    in_specs = [
        pl.BlockSpec((None, nrow, HEAD_DIM), seq_map),
        pl.BlockSpec((None, nrow, HEAD_DIM), seq_map),
        pl.BlockSpec((None, LANES, HEAD_DIM), seq_map),
        pl.BlockSpec((None, LANES, HEAD_DIM), seq_map),
        pl.BlockSpec((None, LANES, HEAD_DIM), seq_map),
        pl.BlockSpec((None, LANES, HEAD_DIM), seq_map),
        pl.BlockSpec((None, SUBLANES, P * CK), lambda s, ci, pt: (s, 0, N_PAGE_STEPS - 1 - ci)),
        pl.BlockSpec((None, SUBLANES, CK), lambda s, ci, pt: (s, 0, N_PAGES)),
        pl.BlockSpec((nrow, 2 * CK), const2),
        pl.BlockSpec((nrow, LANES), const2),
        pl.BlockSpec((2 * CK, 2 * CK), const2),
    ]
    in_specs += [pl.BlockSpec((None, None, rows_b, HEAD_DIM), page_map(i, 4)) for i in range(P)] * 2
    in_specs += [pl.BlockSpec((None, None, rows_a, HEAD_DIM), page_map(i, 4)) for i in range(P)] * 2
    grid_spec = pltpu.PrefetchScalarGridSpec(
        num_scalar_prefetch=1,
        grid=(DEC_BATCH, N_PAGE_STEPS),
        in_specs=in_specs,
        out_specs=[pl.BlockSpec((None, nrow, HEAD_DIM), seq_map)] * 2,
        scratch_shapes=[pltpu.VMEM((nrow, HEAD_DIM), F32), pltpu.VMEM((nrow, CK), F32),
                        pltpu.VMEM((nrow, 1), F32), pltpu.VMEM((nrow, 1), F32),
                        pltpu.VMEM((nrow, HEAD_DIM), F32)],
    )
    return pl.pallas_call(
        _decode_kernel,
        grid_spec=grid_spec,
        out_shape=[jax.ShapeDtypeStruct((DEC_BATCH, nrow, HEAD_DIM), F32)] * 2,
        compiler_params=_cparams(2),
        name="decode_attention_sample",
    )(page_table, qb_s, qa_s, kb_new, vb_new, ka_new, va_new, am_s, am_s, strip, far, uj2,
      *([ckb] * P), *([cvb] * P), *([cka] * P), *([cva] * P))


def _merge_kernel(oa_ref, ga_ref, ob_ref, gb_ref, ma_ref, mb_ref, wpa_ref, wpb_ref, u_ref):
    ga = ga_ref[...]
    gb = gb_ref[...]
    ya = _dot((oa_ref[...] * (ga * jax.nn.sigmoid(ga))).astype(BF16), wpa_ref[...])
    yb = _dot((ob_ref[...] * (gb * jax.nn.sigmoid(gb))).astype(BF16), wpb_ref[...])
    u = jax.nn.sigmoid(ma_ref[...]) * ya + jax.nn.sigmoid(mb_ref[...]) * yb
    u_ref[...] = u.astype(BF16)


def _merge(o_a, o_b, ga, gb, mab, w_pa, w_pb, tm=256):
    return pl.pallas_call(
        _merge_kernel,
        grid=(M_ALL // tm,),
        in_specs=[pl.BlockSpec((tm, W_A), lambda i: (i, 0)),
                  pl.BlockSpec((tm, W_A), lambda i: (i, 0)),
                  pl.BlockSpec((tm, W_B), lambda i: (i, 0)),
                  pl.BlockSpec((tm, W_B), lambda i: (i, 0)),
                  pl.BlockSpec((tm, D_MODEL), lambda i: (i, 0)),
                  pl.BlockSpec((tm, D_MODEL), lambda i: (i, 1)),
                  pl.BlockSpec((W_A, D_MODEL), lambda i: (0, 0)),
                  pl.BlockSpec((W_B, D_MODEL), lambda i: (0, 0))],
        out_specs=pl.BlockSpec((tm, D_MODEL), lambda i: (i, 0)),
        out_shape=jax.ShapeDtypeStruct((M_ALL, D_MODEL), BF16),
        compiler_params=_cparams(1),
        name="gated_merge",
    )(o_a, ga, o_b, gb, mab, mab, w_pa, w_pb)


def _outproj_kernel(u_ref, x_ref, w_ref, g_ref, b_ref, o_ref):
    mix = _dot(u_ref[...], w_ref[...])
    o_ref[...] = _layer_norm_rows(ALPHA * x_ref[...] + mix, g_ref[...], b_ref[...])


def _outproj(u, x, w_out, g, b, tm=256):
    return pl.pallas_call(
        _outproj_kernel,
        grid=(M_ALL // tm,),
        in_specs=[pl.BlockSpec((tm, D_MODEL), lambda i: (i, 0)),
                  pl.BlockSpec((tm, D_MODEL), lambda i: (i, 0)),
                  pl.BlockSpec((D_MODEL, D_MODEL), lambda i: (0, 0)),
                  pl.BlockSpec((1, D_MODEL), lambda i: (0, 0)),
                  pl.BlockSpec((1, D_MODEL), lambda i: (0, 0))],
        out_specs=pl.BlockSpec((tm, D_MODEL), lambda i: (i, 0)),
        out_shape=jax.ShapeDtypeStruct((M_ALL, D_MODEL), F32),
        compiler_params=_cparams(1),
        name="out_proj_norm",
    )(u, x, w_out, g, b)


def _ple_kernel(x_ref, p_ref, wg_ref, we_ref, g_ref, b_ref, o_ref):
    x = x_ref[...]
    gate = jax.nn.sigmoid(_dot(x.astype(BF16), wg_ref[...]))
    ple = gate * _dot(p_ref[...].astype(BF16), we_ref[...])
    o_ref[...] = _layer_norm_rows(ALPHA * x + ple, g_ref[...], b_ref[...])


def _ple(x, p, w_pg, w_pe, g, b, tm=256):
    return pl.pallas_call(
        _ple_kernel,
        grid=(M_ALL // tm,),
        in_specs=[pl.BlockSpec((tm, D_MODEL), lambda i: (i, 0)),
                  pl.BlockSpec((tm, PLE_DIM), lambda i: (i, 0)),
                  pl.BlockSpec((D_MODEL, D_MODEL), lambda i: (0, 0)),
                  pl.BlockSpec((PLE_DIM, D_MODEL), lambda i: (0, 0)),
                  pl.BlockSpec((1, D_MODEL), lambda i: (0, 0)),
                  pl.BlockSpec((1, D_MODEL), lambda i: (0, 0))],
        out_specs=pl.BlockSpec((tm, D_MODEL), lambda i: (i, 0)),
        out_shape=jax.ShapeDtypeStruct((M_ALL, D_MODEL), F32),
        compiler_params=_cparams(1),
        name="ple_update_norm",
    )(x, p, w_pg, w_pe, g, b)


def _tri_constants():
    a = jnp.arange(CK, dtype=I32)
    ustrict = (a[:, None] > a[None, :]).astype(BF16)
    ustrict2 = jnp.concatenate([ustrict, ustrict], axis=0)
    utri = (a[:, None] <= a[None, :]).astype(BF16)
    same_head = (a[:, None] % H_B) == (a[None, :] % H_B)
    u_b = (same_head & ((a[:, None] // H_B) > (a[None, :] // H_B))).astype(BF16)
    j_b = same_head.astype(BF16)
    uj = jnp.concatenate([u_b, j_b], axis=1)
    uj2 = jnp.concatenate([uj, uj], axis=0)
    s = jnp.arange(LANES, dtype=I32)
    dup = (s[:, None] == (a[None, :] // KV_A)).astype(BF16)
    return ustrict2, utri, uj2, dup


def kernel(x_prompt, x_sample, cache_k_a, cache_v_a, cache_kidx, cache_k_b, cache_v_b, page_table,
           p_prompt, p_sample, w_in, ln_kidx_g, ln_kidx_b, rel_bias, w_pa, w_pb, w_out,
           ln1_g, ln1_b, w_pe, w_pg, ln2_g, ln2_b):
    n_pool = cache_k_a.shape[0]
    ckb = cache_k_b.reshape(n_pool, DEPTH, PAGE_SIZE * H_B, HEAD_DIM)
    cvb = cache_v_b.reshape(n_pool, DEPTH, PAGE_SIZE * H_B, HEAD_DIM)
    cka = cache_k_a.reshape(n_pool, DEPTH, PAGE_SIZE * KV_A, HEAD_DIM)
    cva = cache_v_a.reshape(n_pool, DEPTH, PAGE_SIZE * KV_A, HEAD_DIM)

    ustrict2, utri, uj2, dup = _tri_constants()
    bias_tiles, strip, far = _bias_tables(rel_bias)

    kv_w = KV_A * HEAD_DIM
    o_qa, o_kva, o_ga = 0, W_A, W_A + 2 * kv_w
    o_qi = o_ga + W_A
    o_ki = o_qi + H_I * D_IDX
    o_hi = o_ki + D_IDX + H_I
    h_qb, h_kvb, h_gb, h_mab = 0, W_B, 3 * W_B, 4 * W_B
    w_lo = w_in[:, :, :o_ki + LANES].astype(BF16)
    w_hi = w_in[:, :, o_hi:].astype(BF16)

    x = jnp.concatenate([x_prompt.reshape(M_PROMPT, D_MODEL), x_sample.reshape(M_SAMPLE, D_MODEL)], 0)
    p_all = jnp.concatenate([p_prompt.reshape(DEPTH, M_PROMPT, PLE_DIM),
                             p_sample.reshape(DEPTH, M_SAMPLE, PLE_DIM)], axis=1)

    rows_p = {n: [] for n in ("k_a", "v_a", "kidx", "k_b", "v_b")}
    rows_s = {n: [] for n in ("k_a", "v_a", "kidx", "k_b", "v_b")}

    for i in range(DEPTH):
        g2 = jnp.concatenate([ln_kidx_g[i], ln_kidx_g[i]])[None, :]
        b2 = jnp.concatenate([ln_kidx_b[i], ln_kidx_b[i]])[None, :]

        xb = x.astype(BF16)
        (qa,) = _proj(xb, w_lo, i, o_qa, W_A, [BF16])
        kva_f, kva_bf = _proj(xb, w_lo, i, o_kva, 2 * kv_w, [F32, BF16])
        (ga,) = _proj(xb, w_lo, i, o_ga, W_A, [F32])
        (qi,) = _proj(xb, w_lo, i, o_qi, H_I * D_IDX, [BF16])
        kidx2, ki2_bf, wi_all = _kiw(xb, w_lo, i, o_ki, g2, b2)
        (qb,) = _proj(xb, w_hi, i, h_qb, W_B, [BF16])
        kvb_f, kvb_bf = _proj(xb, w_hi, i, h_kvb, 2 * W_B, [F32, BF16])
        (gb,) = _proj(xb, w_hi, i, h_gb, W_B, [F32])
        (mab,) = _proj(xb, w_hi, i, h_mab, 2 * D_MODEL, [F32])

        rows_p["k_a"].append(kva_f[:M_PROMPT, :kv_w].reshape(BATCH, SEQ, KV_A, HEAD_DIM))
        rows_p["v_a"].append(kva_f[:M_PROMPT, kv_w:].reshape(BATCH, SEQ, KV_A, HEAD_DIM))
        rows_p["k_b"].append(kvb_f[:M_PROMPT, :W_B].reshape(BATCH, SEQ, H_B, HEAD_DIM))
        rows_p["v_b"].append(kvb_f[:M_PROMPT, W_B:].reshape(BATCH, SEQ, H_B, HEAD_DIM))
        rows_p["kidx"].append(kidx2[:M_PROMPT, :D_IDX].reshape(BATCH, SEQ, D_IDX))
        rows_s["k_a"].append(kva_f[M_PROMPT:, :kv_w].reshape(DEC_BATCH, DEC_SEQ, KV_A, HEAD_DIM))
        rows_s["v_a"].append(kva_f[M_PROMPT:, kv_w:].reshape(DEC_BATCH, DEC_SEQ, KV_A, HEAD_DIM))
        rows_s["k_b"].append(kvb_f[M_PROMPT:, :W_B].reshape(DEC_BATCH, DEC_SEQ, H_B, HEAD_DIM))
        rows_s["v_b"].append(kvb_f[M_PROMPT:, W_B:].reshape(DEC_BATCH, DEC_SEQ, H_B, HEAD_DIM))
        rows_s["kidx"].append(kidx2[M_PROMPT:, :D_IDX].reshape(DEC_BATCH, DEC_SEQ, D_IDX))

        oa_p = _dsa_prompt(qa, qi, wi_all, ki2_bf, kva_bf, bias_tiles, utri)
        ob_p = _sb_prompt(qb, kvb_bf, ustrict2)

        kvs_bf = jnp.concatenate([kva_bf[M_PROMPT:], kvb_bf[M_PROMPT:]], axis=1)
        qa_s = qa[M_PROMPT:].reshape(DEC_BATCH, DEC_SEQ * H_A, HEAD_DIM)
        qb_s = qb[M_PROMPT:].reshape(DEC_BATCH, DEC_SEQ * H_B, HEAD_DIM)
        qi_s = qi[M_PROMPT:].reshape(DEC_BATCH, DEC_SEQ * H_I, D_IDX)
        qi_s = jnp.pad(qi_s, ((0, 0), (0, LANES - DEC_SEQ * H_I), (0, 0)))
        wi_s = wi_all[M_PROMPT:, D_IDX:D_IDX + H_I].reshape(DEC_BATCH, DEC_SEQ * H_I, 1)
        wi_s = jnp.pad(wi_s, ((0, 0), (0, LANES - DEC_SEQ * H_I), (0, 0)))
        kidx_new = ki2_bf[M_PROMPT:, :D_IDX].reshape(DEC_BATCH, DEC_SEQ, D_IDX)
        kidx_new = jnp.pad(kidx_new, ((0, 0), (0, LANES - DEC_SEQ), (0, 0)))

        def new_tile(a, heads, slots):
            a = a.reshape(DEC_BATCH, DEC_SEQ, heads, HEAD_DIM)
            a = jnp.pad(a, ((0, 0), (0, slots - DEC_SEQ), (0, 0), (0, 0)))
            return a.reshape(DEC_BATCH, slots * heads, HEAD_DIM)

        ka_new = new_tile(kvs_bf[:, 0:256], KV_A, NEW_A)
        va_new = new_tile(kvs_bf[:, 256:512], KV_A, NEW_A)
        kb_new = new_tile(kvs_bf[:, 512:512 + W_B], H_B, NEW_B)
        vb_new = new_tile(kvs_bf[:, 512 + W_B:], H_B, NEW_B)

        am = _sel_sample(page_table, i, qi_s, wi_s, cache_kidx, kidx_new, dup, utri[:LANES, :LANES])
        am_s = am.reshape(DEC_BATCH, SUBLANES, S_TILES * CK)
        ob_s, oa_s = _decode(page_table, i, qb_s, qa_s, kb_new, vb_new, ka_new, va_new, am_s,
                             strip, far, uj2, ckb, cvb, cka, cva)

        o_a = lax.dynamic_update_slice(oa_p, oa_s.reshape(M_SAMPLE, W_A), (M_PROMPT, 0))
        o_b = lax.dynamic_update_slice(ob_p, ob_s.reshape(M_SAMPLE, W_B), (M_PROMPT, 0))

        u = _merge(o_a, o_b, ga, gb, mab, w_pa[i].astype(BF16), w_pb[i].astype(BF16))
        x = _outproj(u, x, w_out[i].astype(BF16), ln1_g[i][None, :], ln1_b[i][None, :])
        x = _ple(x, p_all[i], w_pg[i].astype(BF16), w_pe[i].astype(BF16),
                 ln2_g[i][None, :], ln2_b[i][None, :])

    st = lambda rows: jnp.stack(rows, axis=1)
    return (x[:M_PROMPT].reshape(BATCH, SEQ, D_MODEL), x[M_PROMPT:].reshape(DEC_BATCH, DEC_SEQ, D_MODEL),
            st(rows_p["k_a"]), st(rows_p["v_a"]), st(rows_p["kidx"]), st(rows_p["k_b"]), st(rows_p["v_b"]),
            st(rows_s["k_a"]), st(rows_s["v_a"]), st(rows_s["kidx"]), st(rows_s["k_b"]), st(rows_s["v_b"]))
```

```python
import functools
import math

import jax
import jax.numpy as jnp
from jax import lax
from jax.experimental import pallas as pl
from jax.experimental.pallas import tpu as pltpu

F32 = jnp.float32
BF16 = jnp.bfloat16
I32 = jnp.int32

D_MODEL = 2048
BATCH = 4
SEQ = 2048
DEPTH = 2
DEC_BATCH = 128
DEC_SEQ = 4
PAST_LEN = 2048
PAGE_SIZE = 128
N_PAGES = PAST_LEN // PAGE_SIZE
HEAD_DIM = 128
H_A = 8
KV_A = 2
G_A = H_A // KV_A
W_A = H_A * HEAD_DIM
H_I = 16
D_IDX = 64
TOPK = 256
H_B = 8
W_B = H_B * HEAD_DIM
N_BUCKETS = 32
MAX_DISTANCE = 128
PLE_DIM = 256
LN_EPS = 1e-5
ALPHA = (2 * DEPTH) ** 0.25

M_PROMPT = BATCH * SEQ
M_SAMPLE = DEC_BATCH * DEC_SEQ
M_ALL = M_PROMPT + M_SAMPLE

LANES = 128
SUBLANES = 8
Q_BLK = 128
CK = 256
N_QBLK = SEQ // Q_BLK
NEG = -1e30
INT_MIN = -2 ** 31
ATT_SCALE = HEAD_DIM ** -0.5
VMEM_LIMIT = 52 * 1024 * 1024

SEQ_GROUP = 32
N_SEQ_GROUPS = DEC_BATCH // SEQ_GROUP
S_TILES = N_PAGES + 1
PAGES_PER_STEP = 8
N_PAGE_STEPS = N_PAGES // PAGES_PER_STEP
PROJ_TM = M_ALL // 4
PROJ_TN = 512
SB_HEADS_PER_STEP = 8
DSA_Q_BLK = 256
SEARCH_BITS_PER_CHECK = 4
NEW_B = LANES // H_B
NEW_A = LANES // KV_A


def _cparams(n_axes):
    return pltpu.CompilerParams(dimension_semantics=("arbitrary",) * n_axes,
                                vmem_limit_bytes=VMEM_LIMIT)


def _dot_nt(a, b):
    return lax.dot_general(a, b, (((1,), (1,)), ((), ())), preferred_element_type=F32)


def _dot(a, b):
    return jnp.dot(a, b, preferred_element_type=F32)


def _sortable(s):
    b = pltpu.bitcast(s, I32)
    return jnp.where(b < 0, b ^ jnp.int32(0x7FFFFFFF), b)


def _hi_lo(x):
    hi = x.astype(BF16)
    lo = (x - hi.astype(F32)).astype(BF16)
    return jnp.concatenate([hi, lo], axis=1)


def _softplus(z):
    return jnp.maximum(z, 0.0) + jnp.log(1.0 + jnp.exp(-jnp.abs(z)))


def _layer_norm_rows(x, g, b):
    mu = jnp.mean(x, axis=-1, keepdims=True)
    xc = x - mu
    var = jnp.mean(xc * xc, axis=-1, keepdims=True)
    return xc * lax.rsqrt(var + LN_EPS) * g + b


def _mm_kernel(x_ref, w_ref, *o_refs):
    acc = _dot(x_ref[...], w_ref[...])
    for o in o_refs:
        o[...] = acc.astype(o.dtype)


def _proj(x, w, layer, col0, n, out_dtypes):
    m, k = x.shape
    tm, tn = PROJ_TM, PROJ_TN
    cb0 = col0 // tn
    return pl.pallas_call(
        _mm_kernel,
        grid=(m // tm, n // tn),
        in_specs=[pl.BlockSpec((tm, k), lambda i, j: (i, 0)),
                  pl.BlockSpec((None, k, tn), lambda i, j: (layer, 0, cb0 + j))],
        out_specs=[pl.BlockSpec((tm, tn), lambda i, j: (i, j)) for _ in out_dtypes],
        out_shape=[jax.ShapeDtypeStruct((m, n), d) for d in out_dtypes],
        compiler_params=_cparams(2),
        name="proj_matmul",
    )(x, w)


def _kiw_kernel(x_ref, w_ref, g_ref, b_ref, kf_ref, kb_ref, wi_ref):
    h = _dot(x_ref[...], w_ref[...])
    lane = lax.broadcasted_iota(I32, h.shape, 1)
    is_k = lane < D_IDX
    mu = jnp.sum(jnp.where(is_k, h, 0.0), axis=-1, keepdims=True) * (1.0 / D_IDX)
    hc = jnp.where(is_k, h - mu, 0.0)
    var = jnp.sum(hc * hc, axis=-1, keepdims=True) * (1.0 / D_IDX)
    kn = hc * lax.rsqrt(var + LN_EPS)
    k2 = jnp.where(is_k, kn, pltpu.roll(kn, D_IDX, 1)) * g_ref[...] + b_ref[...]
    kf_ref[...] = k2
    kb_ref[...] = k2.astype(BF16)
    wi_ref[...] = h * (H_I ** -0.5 * D_IDX ** -0.5)


def _kiw(x, w, layer, col0, g2, b2, tm=512):
    m, k = x.shape
    return pl.pallas_call(
        _kiw_kernel,
        grid=(m // tm,),
        in_specs=[pl.BlockSpec((tm, k), lambda i: (i, 0)),
                  pl.BlockSpec((None, k, LANES), lambda i: (layer, 0, col0 // LANES)),
                  pl.BlockSpec((1, LANES), lambda i: (0, 0)),
                  pl.BlockSpec((1, LANES), lambda i: (0, 0))],
        out_specs=[pl.BlockSpec((tm, LANES), lambda i: (i, 0))] * 3,
        out_shape=[jax.ShapeDtypeStruct((m, LANES), F32),
                   jax.ShapeDtypeStruct((m, LANES), BF16),
                   jax.ShapeDtypeStruct((m, LANES), F32)],
        compiler_params=_cparams(1),
        name="indexer_key_proj",
    )(x, w, g2, b2)


def _t5_bucket(d):
    n = jnp.maximum(d, 0)
    max_exact = N_BUCKETS // 2
    nf = jnp.maximum(n, 1).astype(F32)
    large = max_exact + (jnp.log(nf / max_exact) / math.log(MAX_DISTANCE / max_exact)
                         * (N_BUCKETS - max_exact)).astype(I32)
    return jnp.where(n < max_exact, n, jnp.minimum(large, N_BUCKETS - 1))


def _bias_kernel(rb_ref, tiles_ref, strip_ref, far_ref):
    ii = lax.broadcasted_iota(I32, (Q_BLK, LANES), 0)
    jj = lax.broadcasted_iota(I32, (Q_BLK, LANES), 1)
    bk0 = _t5_bucket(ii - jj)
    bk1 = _t5_bucket(ii - jj + LANES)
    for n in range(H_A):
        t0 = jnp.zeros((Q_BLK, LANES), F32)
        t1 = jnp.zeros((Q_BLK, LANES), F32)
        for b in range(N_BUCKETS):
            t0 = jnp.where(bk0 == b, rb_ref[b, n], t0)
            t1 = jnp.where(bk1 == b, rb_ref[b, n], t1)
        tiles_ref[n, 0] = t0
        tiles_ref[n, 1] = t1
        tiles_ref[n, 2] = jnp.full((Q_BLK, LANES), rb_ref[N_BUCKETS - 1, n], F32)
    rows = lax.broadcasted_iota(I32, (DEC_SEQ * H_A, 2 * CK), 0)
    lan = lax.broadcasted_iota(I32, (DEC_SEQ * H_A, 2 * CK), 1)
    q = rows // H_A
    n_of_row = rows % H_A
    pos = jnp.where(lan < CK, PAST_LEN - PAGE_SIZE + lan // KV_A, PAST_LEN + (lan - CK) // KV_A)
    bks = _t5_bucket(PAST_LEN + q - pos)
    strip = jnp.zeros((DEC_SEQ * H_A, 2 * CK), F32)
    far = jnp.zeros((DEC_SEQ * H_A, LANES), F32)
    rows_f = lax.broadcasted_iota(I32, (DEC_SEQ * H_A, LANES), 0) % H_A
    for n in range(H_A):
        sn = jnp.zeros((DEC_SEQ * H_A, 2 * CK), F32)
        for b in range(N_BUCKETS):
            sn = jnp.where(bks == b, rb_ref[b, n], sn)
        strip = jnp.where(n_of_row == n, sn, strip)
        far = jnp.where(rows_f == n, rb_ref[N_BUCKETS - 1, n], far)
    strip_ref[...] = strip
    far_ref[...] = far


def _bias_tables(rel_bias):
    return pl.pallas_call(
        _bias_kernel,
        in_specs=[pl.BlockSpec(memory_space=pltpu.SMEM)],
        out_shape=[jax.ShapeDtypeStruct((H_A, 3, Q_BLK, LANES), F32),
                   jax.ShapeDtypeStruct((DEC_SEQ * H_A, 2 * CK), F32),
                   jax.ShapeDtypeStruct((DEC_SEQ * H_A, LANES), F32)],
        name="t5_bias_tables",
    )(rel_bias)


def _topk_threshold(count_ge, n_keys, few):
    rows = few.shape[0]
    t_init = jnp.full((rows, 1), INT_MIN, I32)
    cnt_init = jnp.zeros((rows, 1), F32) + n_keys
    passes_per_check = SEARCH_BITS_PER_CHECK // 2

    def pending(cnt):
        return (jnp.max(jnp.where(few, 0.0, jnp.abs(cnt - TOPK))) > 0.0).astype(I32)

    def steps(state):
        g, t, cnt, _ = state
        for b in range(passes_per_check):
            i = g * passes_per_check + b
            hi = jnp.left_shift(jnp.int32(1), 31 - 2 * i)
            lo = jnp.left_shift(jnp.int32(1), 30 - 2 * i)
            cands = [t + lo, t + hi, t + hi + lo]
            counts = count_ge(cands)
            for cand, c in zip(cands, counts):
                ok = c >= TOPK
                t = jnp.where(ok, cand, t)
                cnt = jnp.where(ok, c, cnt)
        return g + 1, t, cnt, pending(cnt)

    def cond(state):
        g, _, _, flag = state
        return (g < 32 // SEARCH_BITS_PER_CHECK) & (flag > 0)

    _, t, _, _ = lax.while_loop(cond, steps, (jnp.int32(0), t_init, cnt_init, pending(cnt_init)))
    return jnp.where(few, jnp.int32(INT_MIN), t)


def _dsa_prompt_kernel(qa_ref, qi_ref, wi_ref, ki2_ref, ka_ref, va_ref, bias_ref, utri_ref,
                       o_ref, qim_ref, qg_ref, key_ref, am_ref, m_ref, acc_ref):
    Q_BLK = DSA_Q_BLK
    tiles_per_blk = Q_BLK // LANES
    j = pl.program_id(1)
    nck = ((j + 1) * Q_BLK + CK - 1) // CK
    t0 = j * Q_BLK

    lane = lax.broadcasted_iota(I32, (Q_BLK, LANES), 1)
    for pr in range(H_I // 2):
        qp = qi_ref[:, pr * LANES:(pr + 1) * LANES].astype(F32)
        qim_ref[(2 * pr) * Q_BLK:(2 * pr + 1) * Q_BLK, :] = jnp.where(lane < D_IDX, qp, 0.0).astype(BF16)
        qim_ref[(2 * pr + 1) * Q_BLK:(2 * pr + 2) * Q_BLK, :] = jnp.where(lane >= D_IDX, qp, 0.0).astype(BF16)
    for n in range(H_A):
        g, nl = divmod(n, G_A)
        qg_ref[g, nl * Q_BLK:(nl + 1) * Q_BLK, :] = qa_ref[:, n * HEAD_DIM:(n + 1) * HEAD_DIM]

    rows = t0 + lax.broadcasted_iota(I32, (Q_BLK, CK), 0)
    cols0 = lax.broadcasted_iota(I32, (Q_BLK, CK), 1)
    wi = wi_ref[...]

    def score_chunk(c, carry):
        kc = ki2_ref[pl.ds(pl.multiple_of(c * CK, CK), CK), :]
        d = _dot_nt(qim_ref[...], kc)
        s = jnp.zeros((Q_BLK, CK), F32)
        for h in range(H_I):
            s = s + wi[:, D_IDX + h:D_IDX + h + 1] * jnp.maximum(d[h * Q_BLK:(h + 1) * Q_BLK], 0.0)
        adm = (cols0 + c * CK) <= rows
        key_ref[c] = _sortable(jnp.where(adm, s, -jnp.inf))
        return carry

    lax.fori_loop(0, nck, score_chunk, 0)

    def count_ge(cands):
        slabs = []
        for r0 in range(0, Q_BLK, LANES):
            cbs = [jnp.broadcast_to(cand[r0:r0 + LANES], (LANES, LANES)) for cand in cands]

            def body(c, accs, r0=r0, cbs=cbs):
                k0 = key_ref[c, r0:r0 + LANES, :LANES]
                k1 = key_ref[c, r0:r0 + LANES, LANES:]
                return tuple(acc + jnp.where(k0 >= cb, 1.0, 0.0) + jnp.where(k1 >= cb, 1.0, 0.0)
                             for acc, cb in zip(accs, cbs))

            slabs.append(lax.fori_loop(0, nck, body,
                                       tuple(jnp.zeros((LANES, LANES), F32) for _ in cands)))
        return [jnp.sum(jnp.concatenate([accs[k] for accs in slabs], axis=0), axis=1, keepdims=True)
                for k in range(len(cands))]

    few = (t0 + lax.broadcasted_iota(I32, (Q_BLK, 1), 0)) < TOPK
    thr = _topk_threshold(count_ge, (nck * CK).astype(F32), few)
    tb = jnp.broadcast_to(thr, (Q_BLK, CK))

    def mask_chunk(c, acc):
        adm = (cols0 + c * CK) <= rows
        sel = jnp.where(adm, jnp.where(key_ref[c] >= tb, 1.0, 0.0), 0.0)
        am_ref[c] = jnp.where(sel > 0.0, 0.0, NEG)
        return acc + sel

    n_sel = jnp.sum(lax.fori_loop(0, nck, mask_chunk, jnp.zeros((Q_BLK, CK), F32)),
                    axis=1, keepdims=True)

    @pl.when(jnp.max(n_sel) > TOPK)
    def _():
        def gt_chunk(c, acc):
            adm = (cols0 + c * CK) <= rows
            return acc + jnp.where(adm, jnp.where(key_ref[c] > tb, 1.0, 0.0), 0.0)

        n_gt = jnp.sum(lax.fori_loop(0, nck, gt_chunk, jnp.zeros((Q_BLK, CK), F32)),
                       axis=1, keepdims=True)
        keep = TOPK - n_gt

        def tie_chunk(c, seen):
            adm = (cols0 + c * CK) <= rows
            key = key_ref[c]
            eq = jnp.where(adm, jnp.where(key == tb, 1.0, 0.0), 0.0)
            rank = seen + _dot(eq.astype(BF16), utri_ref[...])
            sel = jnp.where(adm, jnp.where(key > tb, 1.0, 0.0), 0.0)
            sel = jnp.where(eq > 0.0, jnp.where(rank <= keep, 1.0, 0.0), sel)
            am_ref[c] = jnp.where(sel > 0.0, 0.0, NEG)
            return seen + jnp.sum(eq, axis=1, keepdims=True)

        lax.fori_loop(0, nck, tie_chunk, jnp.zeros((Q_BLK, 1), F32))

    grows = G_A * Q_BLK

    def chunk_logits(c, g):
        start = pl.multiple_of(c * CK, CK)
        am4 = jnp.concatenate([am_ref[c]] * G_A, axis=0)
        kc = ka_ref[pl.ds(start, CK), g * HEAD_DIM:(g + 1) * HEAD_DIM]
        rows_of_tiles = []
        for nl in range(G_A):
            for rt in range(tiles_per_blk):
                dist = [jnp.clip(j * tiles_per_blk + rt - (CK // LANES) * c - ct, 0, 2)
                        for ct in range(CK // LANES)]
                rows_of_tiles.append(
                    jnp.concatenate([bias_ref[g * G_A + nl, d] for d in dist], axis=1))
        bias = jnp.concatenate(rows_of_tiles, axis=0)
        return _dot_nt(qg_ref[g], kc) * ATT_SCALE + bias + am4

    def max_chunk(c, ms):
        out = []
        for g in range(KV_A):
            lg = chunk_logits(c, g)
            out.append(jnp.maximum(ms[g], jnp.maximum(lg[:, :LANES], lg[:, LANES:])))
        return tuple(out)

    ms = lax.fori_loop(0, nck, max_chunk,
                       tuple(jnp.full((grows, LANES), -jnp.inf, F32) for _ in range(KV_A)))
    for g in range(KV_A):
        m_ref[g] = jnp.broadcast_to(jnp.max(ms[g], axis=1, keepdims=True), (grows, LANES))
    acc_ref[...] = jnp.zeros((KV_A, grows, 2 * HEAD_DIM), F32)
    ones = jnp.ones((CK, HEAD_DIM), BF16)

    def acc_chunk(c, carry):
        start = pl.multiple_of(c * CK, CK)
        for g in range(KV_A):
            m = m_ref[g]
            p = jnp.exp(chunk_logits(c, g) - jnp.concatenate([m, m], axis=1))
            v1 = jnp.concatenate([va_ref[pl.ds(start, CK), g * HEAD_DIM:(g + 1) * HEAD_DIM], ones], axis=1)
            acc_ref[g] = acc_ref[g] + _dot(p.astype(BF16), v1)
        return carry

    lax.fori_loop(0, nck, acc_chunk, 0)
    for n in range(H_A):
        g, nl = divmod(n, G_A)
        acc = acc_ref[g, nl * Q_BLK:(nl + 1) * Q_BLK, :]
        o_ref[:, n * HEAD_DIM:(n + 1) * HEAD_DIM] = acc[:, :HEAD_DIM] / acc[:, HEAD_DIM:]


def _dsa_prompt(qa, qi, wi_all, ki2, kva_bf, bias_tiles, utri):
    Q_BLK = DSA_Q_BLK
    rb = SEQ // Q_BLK
    grows = G_A * Q_BLK
    return pl.pallas_call(
        _dsa_prompt_kernel,
        grid=(BATCH, rb),
        in_specs=[
            pl.BlockSpec((Q_BLK, W_A), lambda b, j: (b * rb + j, 0)),
            pl.BlockSpec((Q_BLK, H_I * D_IDX), lambda b, j: (b * rb + j, 0)),
            pl.BlockSpec((Q_BLK, LANES), lambda b, j: (b * rb + j, 0)),
            pl.BlockSpec((SEQ, LANES), lambda b, j: (b, 0)),
            pl.BlockSpec((SEQ, KV_A * HEAD_DIM), lambda b, j: (b, 0)),
            pl.BlockSpec((SEQ, KV_A * HEAD_DIM), lambda b, j: (b, 1)),
            pl.BlockSpec((H_A, 3, LANES, LANES), lambda b, j: (0, 0, 0, 0)),
            pl.BlockSpec((CK, CK), lambda b, j: (0, 0)),
        ],
        out_specs=pl.BlockSpec((Q_BLK, W_A), lambda b, j: (b * rb + j, 0)),
        out_shape=jax.ShapeDtypeStruct((M_ALL, W_A), F32),
        scratch_shapes=[pltpu.VMEM((H_I * Q_BLK, LANES), BF16),
                        pltpu.VMEM((KV_A, grows, HEAD_DIM), BF16),
                        pltpu.VMEM((SEQ // CK, Q_BLK, CK), I32),
                        pltpu.VMEM((SEQ // CK, Q_BLK, CK), F32),
                        pltpu.VMEM((KV_A, grows, LANES), F32),
                        pltpu.VMEM((KV_A, grows, 2 * HEAD_DIM), F32)],
        compiler_params=_cparams(2),
        name="dsa_prompt",
    )(qa, qi, wi_all, ki2, kva_bf, kva_bf, bias_tiles, utri)


def _sb_prompt_kernel(q_ref, k_ref, v_ref, u2_ref, o_ref):
    nh = SB_HEADS_PER_STEP
    j = pl.program_id(2)
    nck = j // 2 + 1
    rows = j * Q_BLK + lax.broadcasted_iota(I32, (Q_BLK, CK), 0)
    cols0 = lax.broadcasted_iota(I32, (Q_BLK, CK), 1)

    def body(i, carry):
        tots, accs = carry
        c = nck - 1 - i
        start = pl.multiple_of(c * CK, CK)
        vis = (cols0 + c * CK) < rows
        zs, sps, lms = [], [], []
        for hh in range(nh):
            hs = slice(hh * HEAD_DIM, (hh + 1) * HEAD_DIM)
            z = _dot_nt(q_ref[:, hs], k_ref[pl.ds(start, CK), hs]) * ATT_SCALE
            sp = _softplus(z)
            zs.append(z)
            sps.append(sp)
            lms.append(jnp.where(vis, -sp, 0.0))
        within = _dot(jnp.concatenate([_hi_lo(lm) for lm in lms], axis=0), u2_ref[...])
        new_tots, new_accs = [], []
        for hh in range(nh):
            hs = slice(hh * HEAD_DIM, (hh + 1) * HEAD_DIM)
            suf = within[hh * Q_BLK:(hh + 1) * Q_BLK] + tots[hh]
            a = jnp.where(vis, jnp.exp(zs[hh] - sps[hh] + suf), 0.0)
            new_accs.append(accs[hh] + _dot(a.astype(BF16), v_ref[pl.ds(start, CK), hs]))
            new_tots.append(tots[hh] + jnp.sum(lms[hh], axis=1, keepdims=True))
        return tuple(new_tots), tuple(new_accs)

    init = (tuple(jnp.zeros((Q_BLK, 1), F32) for _ in range(nh)),
            tuple(jnp.zeros((Q_BLK, HEAD_DIM), F32) for _ in range(nh)))
    _, accs = lax.fori_loop(0, nck, body, init)
    for hh in range(nh):
        o_ref[:, hh * HEAD_DIM:(hh + 1) * HEAD_DIM] = accs[hh]


def _sb_prompt(qb, kvb_bf, ustrict2):
    rb = SEQ // Q_BLK
    nh = SB_HEADS_PER_STEP
    w = nh * HEAD_DIM
    vb_col0 = W_B // w
    return pl.pallas_call(
        _sb_prompt_kernel,
        grid=(BATCH, H_B // nh, N_QBLK),
        in_specs=[
            pl.BlockSpec((Q_BLK, w), lambda b, h, j: (b * rb + j, h)),
            pl.BlockSpec((SEQ, w), lambda b, h, j: (b, h)),
            pl.BlockSpec((SEQ, w), lambda b, h, j: (b, vb_col0 + h)),
            pl.BlockSpec((2 * CK, CK), lambda b, h, j: (0, 0)),
        ],
        out_specs=pl.BlockSpec((Q_BLK, w), lambda b, h, j: (b * rb + j, h)),
        out_shape=jax.ShapeDtypeStruct((M_ALL, W_B), F32),
        compiler_params=_cparams(3),
        name="stick_breaking_prompt",
    )(qb, kvb_bf, kvb_bf, ustrict2)


def _sel_sample_kernel(pt_ref, qi_ref, wi_ref, *rest):
    page_refs = rest[:N_PAGES]
    knew_ref, dup_ref, utri_ref, am_ref, key_ref = rest[N_PAGES:]
    s = pl.program_id(1)
    rows8 = lax.broadcasted_iota(I32, (SUBLANES, LANES), 0)
    cols8 = lax.broadcasted_iota(I32, (SUBLANES, LANES), 1)
    qi = qi_ref[...]
    wcol = wi_ref[...]
    row0 = pl.multiple_of(s * SUBLANES, SUBLANES)
    for p in range(S_TILES):
        if p < N_PAGES:
            dots = _dot(qi, page_refs[p][...].astype(BF16))
            adm = rows8 < DEC_SEQ
        else:
            dots = _dot_nt(qi, knew_ref[...])
            adm = (rows8 < DEC_SEQ) & (cols8 <= rows8)
        r = jnp.maximum(dots, 0.0) * wcol
        sc = jnp.sum(r.reshape(SUBLANES, H_I, LANES), axis=1)
        key_ref[pl.ds(row0, SUBLANES), p * LANES:(p + 1) * LANES] = _sortable(
            jnp.where(adm, sc, -jnp.inf))

    @pl.when(s == SEQ_GROUP - 1)
    def _():
        nrow = SEQ_GROUP * SUBLANES
        rowsg = lax.broadcasted_iota(I32, (nrow, LANES), 0) % SUBLANES
        colsg = lax.broadcasted_iota(I32, (nrow, LANES), 1)
        adm_past = rowsg < DEC_SEQ
        adm_new = adm_past & (colsg <= rowsg)

        def adm_of(p):
            return adm_past if p < N_PAGES else adm_new

        def count_ge(cands):
            cbs = [jnp.broadcast_to(cand, (nrow, LANES)) for cand in cands]
            accs = [jnp.zeros((nrow, LANES), F32) for _ in cands]
            for p in range(S_TILES):
                key = key_ref[:, p * LANES:(p + 1) * LANES]
                accs = [acc + jnp.where(key >= cb, 1.0, 0.0) for acc, cb in zip(accs, cbs)]
            return [jnp.sum(acc, axis=1, keepdims=True) for acc in accs]

        pad_rows = (lax.broadcasted_iota(I32, (nrow, 1), 0) % SUBLANES) >= DEC_SEQ
        thr = _topk_threshold(count_ge, float(S_TILES * LANES), pad_rows)
        tb = jnp.broadcast_to(thr, (nrow, LANES))

        def write(p, sel):
            dup = _dot(sel.astype(BF16), dup_ref[...])
            am_ref[:, p * CK:(p + 1) * CK] = jnp.where(dup > 0.5, 0.0, NEG)

        n_sel = jnp.zeros((nrow, LANES), F32)
        for p in range(S_TILES):
            sel = jnp.where(adm_of(p),
                            jnp.where(key_ref[:, p * LANES:(p + 1) * LANES] >= tb, 1.0, 0.0), 0.0)
            n_sel = n_sel + sel
            write(p, sel)
        n_sel = jnp.sum(n_sel, axis=1, keepdims=True)

        @pl.when(jnp.max(n_sel) > TOPK)
        def _():
            n_gt = jnp.zeros((nrow, LANES), F32)
            for p in range(S_TILES):
                n_gt = n_gt + jnp.where(
                    adm_of(p),
                    jnp.where(key_ref[:, p * LANES:(p + 1) * LANES] > tb, 1.0, 0.0), 0.0)
            keep = TOPK - jnp.sum(n_gt, axis=1, keepdims=True)
            seen = jnp.zeros((nrow, 1), F32)
            for p in range(S_TILES):
                key = key_ref[:, p * LANES:(p + 1) * LANES]
                eq = jnp.where(adm_of(p), jnp.where(key == tb, 1.0, 0.0), 0.0)
                rank = seen + _dot(eq.astype(BF16), utri_ref[...])
                sel = jnp.where(adm_of(p), jnp.where(key > tb, 1.0, 0.0), 0.0)
                sel = jnp.where(eq > 0.0, jnp.where(rank <= keep, 1.0, 0.0), sel)
                write(p, sel)
                seen = seen + jnp.sum(eq, axis=1, keepdims=True)


def _sel_sample(page_table, layer, qi_s, wi_s, cache_kidx, kidx_new, dup, utri):
    def page_map(p):
        return lambda g, s, pt: (pt[g * SEQ_GROUP + s, p], layer, 0, 0)

    seq_map = lambda g, s, pt: (g * SEQ_GROUP + s, 0, 0)
    grid_spec = pltpu.PrefetchScalarGridSpec(
        num_scalar_prefetch=1,
        grid=(N_SEQ_GROUPS, SEQ_GROUP),
        in_specs=[pl.BlockSpec((None, LANES, D_IDX), seq_map),
                  pl.BlockSpec((None, LANES, 1), seq_map)]
        + [pl.BlockSpec((None, None, D_IDX, PAGE_SIZE), page_map(p)) for p in range(N_PAGES)]
        + [pl.BlockSpec((None, LANES, D_IDX), seq_map),
           pl.BlockSpec((LANES, CK), lambda g, s, pt: (0, 0)),
           pl.BlockSpec((LANES, LANES), lambda g, s, pt: (0, 0))],
        out_specs=pl.BlockSpec((None, SEQ_GROUP * SUBLANES, S_TILES * CK),
                               lambda g, s, pt: (g, 0, 0)),
        scratch_shapes=[pltpu.VMEM((SEQ_GROUP * SUBLANES, S_TILES * LANES), I32)],
    )
    return pl.pallas_call(
        _sel_sample_kernel,
        grid_spec=grid_spec,
        out_shape=jax.ShapeDtypeStruct((N_SEQ_GROUPS, SEQ_GROUP * SUBLANES, S_TILES * CK), F32),
        compiler_params=_cparams(2),
        name="indexer_select_sample",
    )(page_table, qi_s, wi_s, *([cache_kidx] * N_PAGES), kidx_new, dup, utri)


def _decode_kernel(pt_ref, qb_ref, qa_ref, kbn_ref, vbn_ref, kan_ref, van_ref, am_ref, amn_ref,
                   strip_ref, far_ref, uj2_ref, *rest):
    P = PAGES_PER_STEP
    kb_refs, vb_refs = rest[0:P], rest[P:2 * P]
    ka_refs, va_refs = rest[2 * P:3 * P], rest[3 * P:4 * P]
    ob_ref, oa_ref, accb_ref, carry_ref, m_ref, l_ref, acca_ref = rest[4 * P:]
    ci = pl.program_id(1)
    nrow = DEC_SEQ * H_B

    rows = lax.broadcasted_iota(I32, (nrow, CK), 0)
    lan = lax.broadcasted_iota(I32, (nrow, CK), 1)
    head_b = (lan % H_B) == (rows % H_B)
    group_a = (lan % KV_A) == ((rows % H_A) // G_A)
    qb = qb_ref[...]
    qa = qa_ref[...]

    def am_rows(am8):
        return jnp.concatenate(
            [jnp.broadcast_to(am8[q:q + 1, :], (H_A, am8.shape[1])) for q in range(DEC_SEQ)], axis=0)

    def dsa_update(lg, pv):
        m = m_ref[...]
        m_new = jnp.maximum(m, jnp.max(lg, axis=1, keepdims=True))
        alpha = jnp.exp(m - m_new)
        p = jnp.exp(lg - m_new)
        l_ref[...] = alpha * l_ref[...] + jnp.sum(p, axis=1, keepdims=True)
        acca_ref[...] = alpha * acca_ref[...] + pv(p.astype(BF16))
        m_ref[...] = m_new

    @pl.when(ci == 0)
    def _():
        m_ref[...] = jnp.full((nrow, 1), -jnp.inf, F32)
        l_ref[...] = jnp.zeros((nrow, 1), F32)
        acca_ref[...] = jnp.zeros((nrow, HEAD_DIM), F32)
        rows1 = lax.broadcasted_iota(I32, (nrow, LANES), 0)
        lan1 = lax.broadcasted_iota(I32, (nrow, LANES), 1)
        head_b1 = (lan1 % H_B) == (rows1 % H_B)
        group_a1 = (lan1 % KV_A) == ((rows1 % H_A) // G_A)
        vis = head_b1 & ((lan1 // H_B) < (rows1 // H_B))
        z = _dot_nt(qb, kbn_ref[...]) * ATT_SCALE
        sp = _softplus(z)
        lm = jnp.where(vis, -sp, 0.0)
        hl = _hi_lo(lm)
        u1 = jnp.concatenate([uj2_ref[:LANES, :LANES], uj2_ref[:LANES, :LANES]], axis=0)
        j1 = jnp.concatenate([uj2_ref[:LANES, CK:CK + LANES], uj2_ref[:LANES, CK:CK + LANES]], axis=0)
        a = jnp.where(vis, jnp.exp(z - sp + _dot(hl, u1)), 0.0)
        tot = _dot(hl, j1)
        accb_ref[...] = _dot(a.astype(BF16), vbn_ref[...])
        carry_ref[...] = jnp.concatenate([tot, tot], axis=1)
        lg = _dot_nt(qa, kan_ref[...]) * ATT_SCALE + strip_ref[:, CK:CK + LANES] \
            + am_rows(amn_ref[...])[:, :LANES]
        dsa_update(jnp.where(group_a1, lg, NEG), lambda p: _dot(p, van_ref[...]))

    nsub = PAGE_SIZE * H_B // CK
    blocks = []
    for i in range(P):
        z = _dot_nt(qb, kb_refs[i][...].astype(BF16)) * ATT_SCALE
        for sb in reversed(range(nsub)):
            blocks.append(z[:, sb * CK:(sb + 1) * CK])
    sps = [_softplus(zb) for zb in blocks]
    lms = [jnp.where(head_b, -sp, 0.0) for sp in sps]
    r = _dot(jnp.concatenate([_hi_lo(lm) for lm in lms], axis=0), uj2_ref[...])
    carry = carry_ref[...]
    a_blocks = []
    for k, zb in enumerate(blocks):
        within = r[k * nrow:(k + 1) * nrow, :CK]
        a_blocks.append(jnp.where(head_b, jnp.exp(zb - sps[k] + within + carry), 0.0).astype(BF16))
        carry = carry + r[k * nrow:(k + 1) * nrow, CK:]
    carry_ref[...] = carry
    accb = accb_ref[...]
    for i in range(P):
        a_page = jnp.concatenate(a_blocks[i * nsub:(i + 1) * nsub][::-1], axis=1)
        accb = accb + _dot(a_page, vb_refs[i][...].astype(BF16))
    accb_ref[...] = accb

    far = jnp.broadcast_to(far_ref[...][:, :1], (nrow, CK))
    am_all = am_rows(am_ref[...])
    lgs = []
    for i in range(P):
        lane0 = (P - 1 - i) * CK
        bias = jnp.where(ci == 0, strip_ref[:, :CK], far) if i == 0 else far
        lg = _dot_nt(qa, ka_refs[i][...].astype(BF16)) * ATT_SCALE + bias \
            + am_all[:, lane0:lane0 + CK]
        lgs.append(jnp.where(group_a, lg, NEG))

    def pv_pages(p):
        out = jnp.zeros((nrow, HEAD_DIM), F32)
        for i in range(P):
            out = out + _dot(p[:, i * CK:(i + 1) * CK], va_refs[i][...].astype(BF16))
        return out

    dsa_update(jnp.concatenate(lgs, axis=1), pv_pages)

    @pl.when(ci == N_PAGE_STEPS - 1)
    def _():
        ob_ref[...] = accb_ref[...]
        oa_ref[...] = acca_ref[...] / l_ref[...]


def _decode(page_table, layer, qb_s, qa_s, kb_new, vb_new, ka_new, va_new, am_s, strip, far,
            uj2, ckb, cvb, cka, cva):
    P = PAGES_PER_STEP

    def page_map(i, ndim):
        def f(s, ci, pt):
            return (pt[s, N_PAGES - 1 - (P * ci + i)], layer) + (0,) * (ndim - 2)
        return f

    seq_map = lambda s, ci, pt: (s, 0, 0)
    const2 = lambda s, ci, pt: (0, 0)
    nrow = DEC_SEQ * H_B
    rows_b = PAGE_SIZE * H_B
    rows_a = PAGE_SIZE * KV_A
    in_specs = [
        pl.BlockSpec((None, nrow, HEAD_DIM), seq_map),
        pl.BlockSpec((None, nrow, HEAD_DIM), seq_map),
        pl.BlockSpec((None, LANES, HEAD_DIM), seq_map),
        pl.BlockSpec((None, LANES, HEAD_DIM), seq_map),
        pl.BlockSpec((None, LANES, HEAD_DIM), seq_map),
        pl.BlockSpec((None, LANES, HEAD_DIM), seq_map),
        pl.BlockSpec((None, SUBLANES, P * CK), lambda s, ci, pt: (s, 0, N_PAGE_STEPS - 1 - ci)),
        pl.BlockSpec((None, SUBLANES, CK), lambda s, ci, pt: (s, 0, N_PAGES)),
        pl.BlockSpec((nrow, 2 * CK), const2),
        pl.BlockSpec((nrow, LANES), const2),
        pl.BlockSpec((2 * CK, 2 * CK), const2),
    ]
    in_specs += [pl.BlockSpec((None, None, rows_b, HEAD_DIM), page_map(i, 4)) for i in range(P)] * 2
    in_specs += [pl.BlockSpec((None, None, rows_a, HEAD_DIM), page_map(i, 4)) for i in range(P)] * 2
    grid_spec = pltpu.PrefetchScalarGridSpec(
        num_scalar_prefetch=1,
        grid=(DEC_BATCH, N_PAGE_STEPS),
        in_specs=in_specs,
        out_specs=[pl.BlockSpec((None, nrow, HEAD_DIM), seq_map)] * 2,
        scratch_shapes=[pltpu.VMEM((nrow, HEAD_DIM), F32), pltpu.VMEM((nrow, CK), F32),
                        pltpu.VMEM((nrow, 1), F32), pltpu.VMEM((nrow, 1), F32),
                        pltpu.VMEM((nrow, HEAD_DIM), F32)],
    )
    return pl.pallas_call(
        _decode_kernel,
        grid_spec=grid_spec,
        out_shape=[jax.ShapeDtypeStruct((DEC_BATCH, nrow, HEAD_DIM), F32)] * 2,
        compiler_params=_cparams(2),
        name="decode_attention_sample",
    )(page_table, qb_s, qa_s, kb_new, vb_new, ka_new, va_new, am_s, am_s, strip, far, uj2,
      *([ckb] * P), *([cvb] * P), *([cka] * P), *([cva] * P))


def _merge_kernel(oa_ref, ga_ref, ob_ref, gb_ref, ma_ref, mb_ref, wpa_ref, wpb_ref, u_ref):
    ga = ga_ref[...]
    gb = gb_ref[...]
    ya = _dot((oa_ref[...] * (ga * jax.nn.sigmoid(ga))).astype(BF16), wpa_ref[...])
    yb = _dot((ob_ref[...] * (gb * jax.nn.sigmoid(gb))).astype(BF16), wpb_ref[...])
    u = jax.nn.sigmoid(ma_ref[...]) * ya + jax.nn.sigmoid(mb_ref[...]) * yb
    u_ref[...] = u.astype(BF16)


def _merge(o_a, o_b, ga, gb, mab, w_pa, w_pb, tm=256):
    return pl.pallas_call(
        _merge_kernel,
        grid=(M_ALL // tm,),
        in_specs=[pl.BlockSpec((tm, W_A), lambda i: (i, 0)),
                  pl.BlockSpec((tm, W_A), lambda i: (i, 0)),
                  pl.BlockSpec((tm, W_B), lambda i: (i, 0)),
                  pl.BlockSpec((tm, W_B), lambda i: (i, 0)),
                  pl.BlockSpec((tm, D_MODEL), lambda i: (i, 0)),
                  pl.BlockSpec((tm, D_MODEL), lambda i: (i, 1)),
                  pl.BlockSpec((W_A, D_MODEL), lambda i: (0, 0)),
                  pl.BlockSpec((W_B, D_MODEL), lambda i: (0, 0))],
        out_specs=pl.BlockSpec((tm, D_MODEL), lambda i: (i, 0)),
        out_shape=jax.ShapeDtypeStruct((M_ALL, D_MODEL), BF16),
        compiler_params=_cparams(1),
        name="gated_merge",
    )(o_a, ga, o_b, gb, mab, mab, w_pa, w_pb)


def _outproj_kernel(u_ref, x_ref, w_ref, g_ref, b_ref, o_ref):
    mix = _dot(u_ref[...], w_ref[...])
    o_ref[...] = _layer_norm_rows(ALPHA * x_ref[...] + mix, g_ref[...], b_ref[...])


def _outproj(u, x, w_out, g, b, tm=256):
    return pl.pallas_call(
        _outproj_kernel,
        grid=(M_ALL // tm,),
        in_specs=[pl.BlockSpec((tm, D_MODEL), lambda i: (i, 0)),
                  pl.BlockSpec((tm, D_MODEL), lambda i: (i, 0)),
                  pl.BlockSpec((D_MODEL, D_MODEL), lambda i: (0, 0)),
                  pl.BlockSpec((1, D_MODEL), lambda i: (0, 0)),
                  pl.BlockSpec((1, D_MODEL), lambda i: (0, 0))],
        out_specs=pl.BlockSpec((tm, D_MODEL), lambda i: (i, 0)),
        out_shape=jax.ShapeDtypeStruct((M_ALL, D_MODEL), F32),
        compiler_params=_cparams(1),
        name="out_proj_norm",
    )(u, x, w_out, g, b)


def _ple_kernel(x_ref, p_ref, wg_ref, we_ref, g_ref, b_ref, o_ref):
    x = x_ref[...]
    gate = jax.nn.sigmoid(_dot(x.astype(BF16), wg_ref[...]))
    ple = gate * _dot(p_ref[...].astype(BF16), we_ref[...])
    o_ref[...] = _layer_norm_rows(ALPHA * x + ple, g_ref[...], b_ref[...])


def _ple(x, p, w_pg, w_pe, g, b, tm=256):
    return pl.pallas_call(
        _ple_kernel,
        grid=(M_ALL // tm,),
        in_specs=[pl.BlockSpec((tm, D_MODEL), lambda i: (i, 0)),
                  pl.BlockSpec((tm, PLE_DIM), lambda i: (i, 0)),
                  pl.BlockSpec((D_MODEL, D_MODEL), lambda i: (0, 0)),
                  pl.BlockSpec((PLE_DIM, D_MODEL), lambda i: (0, 0)),
                  pl.BlockSpec((1, D_MODEL), lambda i: (0, 0)),
                  pl.BlockSpec((1, D_MODEL), lambda i: (0, 0))],
        out_specs=pl.BlockSpec((tm, D_MODEL), lambda i: (i, 0)),
        out_shape=jax.ShapeDtypeStruct((M_ALL, D_MODEL), F32),
        compiler_params=_cparams(1),
        name="ple_update_norm",
    )(x, p, w_pg, w_pe, g, b)


def _tri_constants():
    a = jnp.arange(CK, dtype=I32)
    ustrict = (a[:, None] > a[None, :]).astype(BF16)
    ustrict2 = jnp.concatenate([ustrict, ustrict], axis=0)
    utri = (a[:, None] <= a[None, :]).astype(BF16)
    same_head = (a[:, None] % H_B) == (a[None, :] % H_B)
    u_b = (same_head & ((a[:, None] // H_B) > (a[None, :] // H_B))).astype(BF16)
    j_b = same_head.astype(BF16)
    uj = jnp.concatenate([u_b, j_b], axis=1)
    uj2 = jnp.concatenate([uj, uj], axis=0)
    s = jnp.arange(LANES, dtype=I32)
    dup = (s[:, None] == (a[None, :] // KV_A)).astype(BF16)
    return ustrict2, utri, uj2, dup


def kernel(x_prompt, x_sample, cache_k_a, cache_v_a, cache_kidx, cache_k_b, cache_v_b, page_table,
           p_prompt, p_sample, w_in, ln_kidx_g, ln_kidx_b, rel_bias, w_pa, w_pb, w_out,
           ln1_g, ln1_b, w_pe, w_pg, ln2_g, ln2_b):
    n_pool = cache_k_a.shape[0]
    ckb = cache_k_b.reshape(n_pool, DEPTH, PAGE_SIZE * H_B, HEAD_DIM)
    cvb = cache_v_b.reshape(n_pool, DEPTH, PAGE_SIZE * H_B, HEAD_DIM)
    cka = cache_k_a.reshape(n_pool, DEPTH, PAGE_SIZE * KV_A, HEAD_DIM)
    cva = cache_v_a.reshape(n_pool, DEPTH, PAGE_SIZE * KV_A, HEAD_DIM)
    ckidx_t = jnp.swapaxes(cache_kidx, 2, 3)

    ustrict2, utri, uj2, dup = _tri_constants()
    bias_tiles, strip, far = _bias_tables(rel_bias)

    kv_w = KV_A * HEAD_DIM
    o_qa, o_kva, o_ga = 0, W_A, W_A + 2 * kv_w
    o_qi = o_ga + W_A
    o_ki = o_qi + H_I * D_IDX
    o_hi = o_ki + D_IDX + H_I
    h_qb, h_kvb, h_gb, h_mab = 0, W_B, 3 * W_B, 4 * W_B
    w_lo = w_in[:, :, :o_ki + LANES].astype(BF16)
    w_hi = w_in[:, :, o_hi:].astype(BF16)

    x = jnp.concatenate([x_prompt.reshape(M_PROMPT, D_MODEL), x_sample.reshape(M_SAMPLE, D_MODEL)], 0)
    p_all = jnp.concatenate([p_prompt.reshape(DEPTH, M_PROMPT, PLE_DIM),
                             p_sample.reshape(DEPTH, M_SAMPLE, PLE_DIM)], axis=1)

    rows_p = {n: [] for n in ("k_a", "v_a", "kidx", "k_b", "v_b")}
    rows_s = {n: [] for n in ("k_a", "v_a", "kidx", "k_b", "v_b")}

    for i in range(DEPTH):
        g2 = jnp.concatenate([ln_kidx_g[i], ln_kidx_g[i]])[None, :]
        b2 = jnp.concatenate([ln_kidx_b[i], ln_kidx_b[i]])[None, :]

        xb = x.astype(BF16)
        (qa,) = _proj(xb, w_lo, i, o_qa, W_A, [BF16])
        kva_f, kva_bf = _proj(xb, w_lo, i, o_kva, 2 * kv_w, [F32, BF16])
        (ga,) = _proj(xb, w_lo, i, o_ga, W_A, [F32])
        (qi,) = _proj(xb, w_lo, i, o_qi, H_I * D_IDX, [BF16])
        kidx2, ki2_bf, wi_all = _kiw(xb, w_lo, i, o_ki, g2, b2)
        (qb,) = _proj(xb, w_hi, i, h_qb, W_B, [BF16])
        kvb_f, kvb_bf = _proj(xb, w_hi, i, h_kvb, 2 * W_B, [F32, BF16])
        (gb,) = _proj(xb, w_hi, i, h_gb, W_B, [F32])
        (mab,) = _proj(xb, w_hi, i, h_mab, 2 * D_MODEL, [F32])

        rows_p["k_a"].append(kva_f[:M_PROMPT, :kv_w].reshape(BATCH, SEQ, KV_A, HEAD_DIM))
        rows_p["v_a"].append(kva_f[:M_PROMPT, kv_w:].reshape(BATCH, SEQ, KV_A, HEAD_DIM))
        rows_p["k_b"].append(kvb_f[:M_PROMPT, :W_B].reshape(BATCH, SEQ, H_B, HEAD_DIM))
        rows_p["v_b"].append(kvb_f[:M_PROMPT, W_B:].reshape(BATCH, SEQ, H_B, HEAD_DIM))
        rows_p["kidx"].append(kidx2[:M_PROMPT, :D_IDX].reshape(BATCH, SEQ, D_IDX))
        rows_s["k_a"].append(kva_f[M_PROMPT:, :kv_w].reshape(DEC_BATCH, DEC_SEQ, KV_A, HEAD_DIM))
        rows_s["v_a"].append(kva_f[M_PROMPT:, kv_w:].reshape(DEC_BATCH, DEC_SEQ, KV_A, HEAD_DIM))
        rows_s["k_b"].append(kvb_f[M_PROMPT:, :W_B].reshape(DEC_BATCH, DEC_SEQ, H_B, HEAD_DIM))
        rows_s["v_b"].append(kvb_f[M_PROMPT:, W_B:].reshape(DEC_BATCH, DEC_SEQ, H_B, HEAD_DIM))
        rows_s["kidx"].append(kidx2[M_PROMPT:, :D_IDX].reshape(DEC_BATCH, DEC_SEQ, D_IDX))

        oa_p = _dsa_prompt(qa, qi, wi_all, ki2_bf, kva_bf, bias_tiles, utri)
        ob_p = _sb_prompt(qb, kvb_bf, ustrict2)

        kvs_bf = jnp.concatenate([kva_bf[M_PROMPT:], kvb_bf[M_PROMPT:]], axis=1)
        qa_s = qa[M_PROMPT:].reshape(DEC_BATCH, DEC_SEQ * H_A, HEAD_DIM)
        qb_s = qb[M_PROMPT:].reshape(DEC_BATCH, DEC_SEQ * H_B, HEAD_DIM)
        qi_s = qi[M_PROMPT:].reshape(DEC_BATCH, DEC_SEQ * H_I, D_IDX)
        qi_s = jnp.pad(qi_s, ((0, 0), (0, LANES - DEC_SEQ * H_I), (0, 0)))
        wi_s = wi_all[M_PROMPT:, D_IDX:D_IDX + H_I].reshape(DEC_BATCH, DEC_SEQ * H_I, 1)
        wi_s = jnp.pad(wi_s, ((0, 0), (0, LANES - DEC_SEQ * H_I), (0, 0)))
        kidx_new = ki2_bf[M_PROMPT:, :D_IDX].reshape(DEC_BATCH, DEC_SEQ, D_IDX)
        kidx_new = jnp.pad(kidx_new, ((0, 0), (0, LANES - DEC_SEQ), (0, 0)))

        def new_tile(a, heads, slots):
            a = a.reshape(DEC_BATCH, DEC_SEQ, heads, HEAD_DIM)
            a = jnp.pad(a, ((0, 0), (0, slots - DEC_SEQ), (0, 0), (0, 0)))
            return a.reshape(DEC_BATCH, slots * heads, HEAD_DIM)

        ka_new = new_tile(kvs_bf[:, 0:256], KV_A, NEW_A)
        va_new = new_tile(kvs_bf[:, 256:512], KV_A, NEW_A)
        kb_new = new_tile(kvs_bf[:, 512:512 + W_B], H_B, NEW_B)
        vb_new = new_tile(kvs_bf[:, 512 + W_B:], H_B, NEW_B)

        am = _sel_sample(page_table, i, qi_s, wi_s, ckidx_t, kidx_new, dup, utri[:LANES, :LANES])
        am_s = am.reshape(DEC_BATCH, SUBLANES, S_TILES * CK)
        ob_s, oa_s = _decode(page_table, i, qb_s, qa_s, kb_new, vb_new, ka_new, va_new, am_s,
                             strip, far, uj2, ckb, cvb, cka, cva)

        o_a = lax.dynamic_update_slice(oa_p, oa_s.reshape(M_SAMPLE, W_A), (M_PROMPT, 0))
        o_b = lax.dynamic_update_slice(ob_p, ob_s.reshape(M_SAMPLE, W_B), (M_PROMPT, 0))

        u = _merge(o_a, o_b, ga, gb, mab, w_pa[i].astype(BF16), w_pb[i].astype(BF16))
        x = _outproj(u, x, w_out[i].astype(BF16), ln1_g[i][None, :], ln1_b[i][None, :])
        x = _ple(x, p_all[i], w_pg[i].astype(BF16), w_pe[i].astype(BF16),
                 ln2_g[i][None, :], ln2_b[i][None, :])

    st = lambda rows: jnp.stack(rows, axis=1)
    return (x[:M_PROMPT].reshape(BATCH, SEQ, D_MODEL), x[M_PROMPT:].reshape(DEC_BATCH, DEC_SEQ, D_MODEL),
            st(rows_p["k_a"]), st(rows_p["v_a"]), st(rows_p["kidx"]), st(rows_p["k_b"]), st(rows_p["v_b"]),
            st(rows_s["k_a"]), st(rows_s["v_a"]), st(rows_s["kidx"]), st(rows_s["k_b"]), st(rows_s["v_b"]))
```

```python
import functools
import math

import jax
import jax.numpy as jnp
from jax import lax
from jax.experimental import pallas as pl
from jax.experimental.pallas import tpu as pltpu

F32 = jnp.float32
BF16 = jnp.bfloat16
I32 = jnp.int32

D_MODEL = 2048
BATCH = 4
SEQ = 2048
DEPTH = 2
DEC_BATCH = 128
DEC_SEQ = 4
PAST_LEN = 2048
PAGE_SIZE = 128
N_PAGES = PAST_LEN // PAGE_SIZE
HEAD_DIM = 128
H_A = 8
KV_A = 2
G_A = H_A // KV_A
W_A = H_A * HEAD_DIM
H_I = 16
D_IDX = 64
TOPK = 256
H_B = 8
W_B = H_B * HEAD_DIM
N_BUCKETS = 32
MAX_DISTANCE = 128
PLE_DIM = 256
LN_EPS = 1e-5
ALPHA = (2 * DEPTH) ** 0.25

M_PROMPT = BATCH * SEQ
M_SAMPLE = DEC_BATCH * DEC_SEQ
M_ALL = M_PROMPT + M_SAMPLE

LANES = 128
SUBLANES = 8
Q_BLK = 128
CK = 256
N_QBLK = SEQ // Q_BLK
NEG = -1e30
INT_MIN = -2 ** 31
ATT_SCALE = HEAD_DIM ** -0.5
VMEM_LIMIT = 52 * 1024 * 1024

SEQ_GROUP = 32
N_SEQ_GROUPS = DEC_BATCH // SEQ_GROUP
S_TILES = N_PAGES + 1
PAGES_PER_STEP = 8
N_PAGE_STEPS = N_PAGES // PAGES_PER_STEP
PROJ_TM = M_ALL // 4
PROJ_TN = 512
SB_HEADS_PER_STEP = 8
DSA_Q_BLK = 256
SEARCH_BITS_PER_CHECK = 4
NEW_B = LANES // H_B
NEW_A = LANES // KV_A


def _cparams(n_axes):
    return pltpu.CompilerParams(dimension_semantics=("arbitrary",) * n_axes,
                                vmem_limit_bytes=VMEM_LIMIT)


def _dot_nt(a, b):
    return lax.dot_general(a, b, (((1,), (1,)), ((), ())), preferred_element_type=F32)


def _dot(a, b):
    return jnp.dot(a, b, preferred_element_type=F32)


def _sortable(s):
    b = pltpu.bitcast(s, I32)
    return jnp.where(b < 0, b ^ jnp.int32(0x7FFFFFFF), b)


def _hi_lo(x):
    hi = x.astype(BF16)
    lo = (x - hi.astype(F32)).astype(BF16)
    return jnp.concatenate([hi, lo], axis=1)


def _softplus(z):
    return jnp.maximum(z, 0.0) + jnp.log(1.0 + jnp.exp(-jnp.abs(z)))


def _layer_norm_rows(x, g, b):
    mu = jnp.mean(x, axis=-1, keepdims=True)
    xc = x - mu
    var = jnp.mean(xc * xc, axis=-1, keepdims=True)
    return xc * lax.rsqrt(var + LN_EPS) * g + b


def _mm_kernel(x_ref, w_ref, *o_refs):
    acc = _dot_nt(x_ref[...], w_ref[...])
    for o in o_refs:
        o[...] = acc.astype(o.dtype)


def _proj(x, wt, layer, col0, n, out_dtypes):
    m, k = x.shape
    tm, tn = PROJ_TM, PROJ_TN
    cb0 = col0 // tn
    return pl.pallas_call(
        _mm_kernel,
        grid=(m // tm, n // tn),
        in_specs=[pl.BlockSpec((tm, k), lambda i, j: (i, 0)),
                  pl.BlockSpec((None, tn, k), lambda i, j: (layer, cb0 + j, 0))],
        out_specs=[pl.BlockSpec((tm, tn), lambda i, j: (i, j)) for _ in out_dtypes],
        out_shape=[jax.ShapeDtypeStruct((m, n), d) for d in out_dtypes],
        compiler_params=_cparams(2),
        name="proj_matmul",
    )(x, wt)


def _kiw_kernel(x_ref, w_ref, g_ref, b_ref, kf_ref, kb_ref, wi_ref):
    h = _dot_nt(x_ref[...], w_ref[...])
    lane = lax.broadcasted_iota(I32, h.shape, 1)
    is_k = lane < D_IDX
    mu = jnp.sum(jnp.where(is_k, h, 0.0), axis=-1, keepdims=True) * (1.0 / D_IDX)
    hc = jnp.where(is_k, h - mu, 0.0)
    var = jnp.sum(hc * hc, axis=-1, keepdims=True) * (1.0 / D_IDX)
    kn = hc * lax.rsqrt(var + LN_EPS)
    k2 = jnp.where(is_k, kn, pltpu.roll(kn, D_IDX, 1)) * g_ref[...] + b_ref[...]
    kf_ref[...] = k2
    kb_ref[...] = k2.astype(BF16)
    wi_ref[...] = h * (H_I ** -0.5 * D_IDX ** -0.5)


def _kiw(x, w, layer, col0, g2, b2, tm=512):
    m, k = x.shape
    return pl.pallas_call(
        _kiw_kernel,
        grid=(m // tm,),
        in_specs=[pl.BlockSpec((tm, k), lambda i: (i, 0)),
                  pl.BlockSpec((None, LANES, k), lambda i: (layer, col0 // LANES, 0)),
                  pl.BlockSpec((1, LANES), lambda i: (0, 0)),
                  pl.BlockSpec((1, LANES), lambda i: (0, 0))],
        out_specs=[pl.BlockSpec((tm, LANES), lambda i: (i, 0))] * 3,
        out_shape=[jax.ShapeDtypeStruct((m, LANES), F32),
                   jax.ShapeDtypeStruct((m, LANES), BF16),
                   jax.ShapeDtypeStruct((m, LANES), F32)],
        compiler_params=_cparams(1),
        name="indexer_key_proj",
    )(x, w, g2, b2)


def _t5_bucket(d):
    n = jnp.maximum(d, 0)
    max_exact = N_BUCKETS // 2
    nf = jnp.maximum(n, 1).astype(F32)
    large = max_exact + (jnp.log(nf / max_exact) / math.log(MAX_DISTANCE / max_exact)
                         * (N_BUCKETS - max_exact)).astype(I32)
    return jnp.where(n < max_exact, n, jnp.minimum(large, N_BUCKETS - 1))


def _bias_kernel(rb_ref, tiles_ref, strip_ref, far_ref):
    ii = lax.broadcasted_iota(I32, (Q_BLK, LANES), 0)
    jj = lax.broadcasted_iota(I32, (Q_BLK, LANES), 1)
    bk0 = _t5_bucket(ii - jj)
    bk1 = _t5_bucket(ii - jj + LANES)
    for n in range(H_A):
        t0 = jnp.zeros((Q_BLK, LANES), F32)
        t1 = jnp.zeros((Q_BLK, LANES), F32)
        for b in range(N_BUCKETS):
            t0 = jnp.where(bk0 == b, rb_ref[b, n], t0)
            t1 = jnp.where(bk1 == b, rb_ref[b, n], t1)
        tiles_ref[n, 0] = t0
        tiles_ref[n, 1] = t1
        tiles_ref[n, 2] = jnp.full((Q_BLK, LANES), rb_ref[N_BUCKETS - 1, n], F32)
    rows = lax.broadcasted_iota(I32, (DEC_SEQ * H_A, 2 * CK), 0)
    lan = lax.broadcasted_iota(I32, (DEC_SEQ * H_A, 2 * CK), 1)
    q = rows // H_A
    n_of_row = rows % H_A
    pos = jnp.where(lan < CK, PAST_LEN - PAGE_SIZE + lan // KV_A, PAST_LEN + (lan - CK) // KV_A)
    bks = _t5_bucket(PAST_LEN + q - pos)
    strip = jnp.zeros((DEC_SEQ * H_A, 2 * CK), F32)
    far = jnp.zeros((DEC_SEQ * H_A, LANES), F32)
    rows_f = lax.broadcasted_iota(I32, (DEC_SEQ * H_A, LANES), 0) % H_A
    for n in range(H_A):
        sn = jnp.zeros((DEC_SEQ * H_A, 2 * CK), F32)
        for b in range(N_BUCKETS):
            sn = jnp.where(bks == b, rb_ref[b, n], sn)
        strip = jnp.where(n_of_row == n, sn, strip)
        far = jnp.where(rows_f == n, rb_ref[N_BUCKETS - 1, n], far)
    strip_ref[...] = strip
    far_ref[...] = far


def _bias_tables(rel_bias):
    return pl.pallas_call(
        _bias_kernel,
        in_specs=[pl.BlockSpec(memory_space=pltpu.SMEM)],
        out_shape=[jax.ShapeDtypeStruct((H_A, 3, Q_BLK, LANES), F32),
                   jax.ShapeDtypeStruct((DEC_SEQ * H_A, 2 * CK), F32),
                   jax.ShapeDtypeStruct((DEC_SEQ * H_A, LANES), F32)],
        name="t5_bias_tables",
    )(rel_bias)


def _topk_threshold(count_ge, n_keys, few):
    rows = few.shape[0]
    t_init = jnp.full((rows, 1), INT_MIN, I32)
    cnt_init = jnp.zeros((rows, 1), F32) + n_keys
    passes_per_check = SEARCH_BITS_PER_CHECK // 2

    def pending(cnt):
        return (jnp.max(jnp.where(few, 0.0, jnp.abs(cnt - TOPK))) > 0.0).astype(I32)

    def steps(state):
        g, t, cnt, _ = state
        for b in range(passes_per_check):
            i = g * passes_per_check + b
            hi = jnp.left_shift(jnp.int32(1), 31 - 2 * i)
            lo = jnp.left_shift(jnp.int32(1), 30 - 2 * i)
            cands = [t + lo, t + hi, t + hi + lo]
            counts = count_ge(cands)
            for cand, c in zip(cands, counts):
                ok = c >= TOPK
                t = jnp.where(ok, cand, t)
                cnt = jnp.where(ok, c, cnt)
        return g + 1, t, cnt, pending(cnt)

    def cond(state):
        g, _, _, flag = state
        return (g < 32 // SEARCH_BITS_PER_CHECK) & (flag > 0)

    _, t, _, _ = lax.while_loop(cond, steps, (jnp.int32(0), t_init, cnt_init, pending(cnt_init)))
    return jnp.where(few, jnp.int32(INT_MIN), t)


def _dsa_prompt_kernel(qa_ref, qi_ref, wi_ref, ki2_ref, ka_ref, va_ref, bias_ref, utri_ref,
                       o_ref, qim_ref, qg_ref, key_ref, am_ref, m_ref, acc_ref):
    Q_BLK = DSA_Q_BLK
    tiles_per_blk = Q_BLK // LANES
    j = pl.program_id(1)
    nck = ((j + 1) * Q_BLK + CK - 1) // CK
    t0 = j * Q_BLK

    lane = lax.broadcasted_iota(I32, (Q_BLK, LANES), 1)
    for pr in range(H_I // 2):
        qp = qi_ref[:, pr * LANES:(pr + 1) * LANES].astype(F32)
        qim_ref[(2 * pr) * Q_BLK:(2 * pr + 1) * Q_BLK, :] = jnp.where(lane < D_IDX, qp, 0.0).astype(BF16)
        qim_ref[(2 * pr + 1) * Q_BLK:(2 * pr + 2) * Q_BLK, :] = jnp.where(lane >= D_IDX, qp, 0.0).astype(BF16)
    for n in range(H_A):
        g, nl = divmod(n, G_A)
        qg_ref[g, nl * Q_BLK:(nl + 1) * Q_BLK, :] = qa_ref[:, n * HEAD_DIM:(n + 1) * HEAD_DIM]

    rows = t0 + lax.broadcasted_iota(I32, (Q_BLK, CK), 0)
    cols0 = lax.broadcasted_iota(I32, (Q_BLK, CK), 1)
    wi = wi_ref[...]

    def score_chunk(c, carry):
        kc = ki2_ref[pl.ds(pl.multiple_of(c * CK, CK), CK), :]
        d = _dot_nt(qim_ref[...], kc)
        s = jnp.zeros((Q_BLK, CK), F32)
        for h in range(H_I):
            s = s + wi[:, D_IDX + h:D_IDX + h + 1] * jnp.maximum(d[h * Q_BLK:(h + 1) * Q_BLK], 0.0)
        adm = (cols0 + c * CK) <= rows
        key_ref[c] = _sortable(jnp.where(adm, s, -jnp.inf))
        return carry

    lax.fori_loop(0, nck, score_chunk, 0)

    def count_ge(cands):
        slabs = []
        for r0 in range(0, Q_BLK, LANES):
            cbs = [jnp.broadcast_to(cand[r0:r0 + LANES], (LANES, LANES)) for cand in cands]

            def body(c, accs, r0=r0, cbs=cbs):
                k0 = key_ref[c, r0:r0 + LANES, :LANES]
                k1 = key_ref[c, r0:r0 + LANES, LANES:]
                return tuple(acc + jnp.where(k0 >= cb, 1.0, 0.0) + jnp.where(k1 >= cb, 1.0, 0.0)
                             for acc, cb in zip(accs, cbs))

            slabs.append(lax.fori_loop(0, nck, body,
                                       tuple(jnp.zeros((LANES, LANES), F32) for _ in cands)))
        return [jnp.sum(jnp.concatenate([accs[k] for accs in slabs], axis=0), axis=1, keepdims=True)
                for k in range(len(cands))]

    few = (t0 + lax.broadcasted_iota(I32, (Q_BLK, 1), 0)) < TOPK
    thr = _topk_threshold(count_ge, (nck * CK).astype(F32), few)
    tb = jnp.broadcast_to(thr, (Q_BLK, CK))

    def mask_chunk(c, acc):
        adm = (cols0 + c * CK) <= rows
        sel = jnp.where(adm, jnp.where(key_ref[c] >= tb, 1.0, 0.0), 0.0)
        am_ref[c] = jnp.where(sel > 0.0, 0.0, NEG)
        return acc + sel

    n_sel = jnp.sum(lax.fori_loop(0, nck, mask_chunk, jnp.zeros((Q_BLK, CK), F32)),
                    axis=1, keepdims=True)

    @pl.when(jnp.max(n_sel) > TOPK)
    def _():
        def gt_chunk(c, acc):
            adm = (cols0 + c * CK) <= rows
            return acc + jnp.where(adm, jnp.where(key_ref[c] > tb, 1.0, 0.0), 0.0)

        n_gt = jnp.sum(lax.fori_loop(0, nck, gt_chunk, jnp.zeros((Q_BLK, CK), F32)),
                       axis=1, keepdims=True)
        keep = TOPK - n_gt

        def tie_chunk(c, seen):
            adm = (cols0 + c * CK) <= rows
            key = key_ref[c]
            eq = jnp.where(adm, jnp.where(key == tb, 1.0, 0.0), 0.0)
            rank = seen + _dot(eq.astype(BF16), utri_ref[...])
            sel = jnp.where(adm, jnp.where(key > tb, 1.0, 0.0), 0.0)
            sel = jnp.where(eq > 0.0, jnp.where(rank <= keep, 1.0, 0.0), sel)
            am_ref[c] = jnp.where(sel > 0.0, 0.0, NEG)
            return seen + jnp.sum(eq, axis=1, keepdims=True)

        lax.fori_loop(0, nck, tie_chunk, jnp.zeros((Q_BLK, 1), F32))

    grows = G_A * Q_BLK

    def chunk_logits(c, g):
        start = pl.multiple_of(c * CK, CK)
        am4 = jnp.concatenate([am_ref[c]] * G_A, axis=0)
        kc = ka_ref[pl.ds(start, CK), g * HEAD_DIM:(g + 1) * HEAD_DIM]
        rows_of_tiles = []
        for nl in range(G_A):
            for rt in range(tiles_per_blk):
                dist = [jnp.clip(j * tiles_per_blk + rt - (CK // LANES) * c - ct, 0, 2)
                        for ct in range(CK // LANES)]
                rows_of_tiles.append(
                    jnp.concatenate([bias_ref[g * G_A + nl, d] for d in dist], axis=1))
        bias = jnp.concatenate(rows_of_tiles, axis=0)
        return _dot_nt(qg_ref[g], kc) * ATT_SCALE + bias + am4

    def max_chunk(c, ms):
        out = []
        for g in range(KV_A):
            lg = chunk_logits(c, g)
            out.append(jnp.maximum(ms[g], jnp.maximum(lg[:, :LANES], lg[:, LANES:])))
        return tuple(out)

    ms = lax.fori_loop(0, nck, max_chunk,
                       tuple(jnp.full((grows, LANES), -jnp.inf, F32) for _ in range(KV_A)))
    for g in range(KV_A):
        m_ref[g] = jnp.broadcast_to(jnp.max(ms[g], axis=1, keepdims=True), (grows, LANES))
    acc_ref[...] = jnp.zeros((KV_A, grows, 2 * HEAD_DIM), F32)
    ones = jnp.ones((CK, HEAD_DIM), BF16)

    def acc_chunk(c, carry):
        start = pl.multiple_of(c * CK, CK)
        for g in range(KV_A):
            m = m_ref[g]
            p = jnp.exp(chunk_logits(c, g) - jnp.concatenate([m, m], axis=1))
            v1 = jnp.concatenate([va_ref[pl.ds(start, CK), g * HEAD_DIM:(g + 1) * HEAD_DIM], ones], axis=1)
            acc_ref[g] = acc_ref[g] + _dot(p.astype(BF16), v1)
        return carry

    lax.fori_loop(0, nck, acc_chunk, 0)
    for n in range(H_A):
        g, nl = divmod(n, G_A)
        acc = acc_ref[g, nl * Q_BLK:(nl + 1) * Q_BLK, :]
        o_ref[:, n * HEAD_DIM:(n + 1) * HEAD_DIM] = acc[:, :HEAD_DIM] / acc[:, HEAD_DIM:]


def _dsa_prompt(qa, qi, wi_all, ki2, kva_bf, bias_tiles, utri):
    Q_BLK = DSA_Q_BLK
    rb = SEQ // Q_BLK
    grows = G_A * Q_BLK
    return pl.pallas_call(
        _dsa_prompt_kernel,
        grid=(BATCH, rb),
        in_specs=[
            pl.BlockSpec((Q_BLK, W_A), lambda b, j: (b * rb + j, 0)),
            pl.BlockSpec((Q_BLK, H_I * D_IDX), lambda b, j: (b * rb + j, 0)),
            pl.BlockSpec((Q_BLK, LANES), lambda b, j: (b * rb + j, 0)),
            pl.BlockSpec((SEQ, LANES), lambda b, j: (b, 0)),
            pl.BlockSpec((SEQ, KV_A * HEAD_DIM), lambda b, j: (b, 0)),
            pl.BlockSpec((SEQ, KV_A * HEAD_DIM), lambda b, j: (b, 1)),
            pl.BlockSpec((H_A, 3, LANES, LANES), lambda b, j: (0, 0, 0, 0)),
            pl.BlockSpec((CK, CK), lambda b, j: (0, 0)),
        ],
        out_specs=pl.BlockSpec((Q_BLK, W_A), lambda b, j: (b * rb + j, 0)),
        out_shape=jax.ShapeDtypeStruct((M_ALL, W_A), F32),
        scratch_shapes=[pltpu.VMEM((H_I * Q_BLK, LANES), BF16),
                        pltpu.VMEM((KV_A, grows, HEAD_DIM), BF16),
                        pltpu.VMEM((SEQ // CK, Q_BLK, CK), I32),
                        pltpu.VMEM((SEQ // CK, Q_BLK, CK), F32),
                        pltpu.VMEM((KV_A, grows, LANES), F32),
                        pltpu.VMEM((KV_A, grows, 2 * HEAD_DIM), F32)],
        compiler_params=_cparams(2),
        name="dsa_prompt",
    )(qa, qi, wi_all, ki2, kva_bf, kva_bf, bias_tiles, utri)


def _sb_prompt_kernel(q_ref, k_ref, v_ref, u2_ref, o_ref):
    nh = SB_HEADS_PER_STEP
    j = pl.program_id(2)
    nck = j // 2 + 1
    rows = j * Q_BLK + lax.broadcasted_iota(I32, (Q_BLK, CK), 0)
    cols0 = lax.broadcasted_iota(I32, (Q_BLK, CK), 1)

    def body(i, carry):
        tots, accs = carry
        c = nck - 1 - i
        start = pl.multiple_of(c * CK, CK)
        vis = (cols0 + c * CK) < rows
        zs, sps, lms = [], [], []
        for hh in range(nh):
            hs = slice(hh * HEAD_DIM, (hh + 1) * HEAD_DIM)
            z = _dot_nt(q_ref[:, hs], k_ref[pl.ds(start, CK), hs]) * ATT_SCALE
            sp = _softplus(z)
            zs.append(z)
            sps.append(sp)
            lms.append(jnp.where(vis, -sp, 0.0))
        within = _dot(jnp.concatenate([_hi_lo(lm) for lm in lms], axis=0), u2_ref[...])
        new_tots, new_accs = [], []
        for hh in range(nh):
            hs = slice(hh * HEAD_DIM, (hh + 1) * HEAD_DIM)
            suf = within[hh * Q_BLK:(hh + 1) * Q_BLK] + tots[hh]
            a = jnp.where(vis, jnp.exp(zs[hh] - sps[hh] + suf), 0.0)
            new_accs.append(accs[hh] + _dot(a.astype(BF16), v_ref[pl.ds(start, CK), hs]))
            new_tots.append(tots[hh] + jnp.sum(lms[hh], axis=1, keepdims=True))
        return tuple(new_tots), tuple(new_accs)

    init = (tuple(jnp.zeros((Q_BLK, 1), F32) for _ in range(nh)),
            tuple(jnp.zeros((Q_BLK, HEAD_DIM), F32) for _ in range(nh)))
    _, accs = lax.fori_loop(0, nck, body, init)
    for hh in range(nh):
        o_ref[:, hh * HEAD_DIM:(hh + 1) * HEAD_DIM] = accs[hh]


def _sb_prompt(qb, kvb_bf, ustrict2):
    rb = SEQ // Q_BLK
    nh = SB_HEADS_PER_STEP
    w = nh * HEAD_DIM
    vb_col0 = W_B // w
    return pl.pallas_call(
        _sb_prompt_kernel,
        grid=(BATCH, H_B // nh, N_QBLK),
        in_specs=[
            pl.BlockSpec((Q_BLK, w), lambda b, h, j: (b * rb + j, h)),
            pl.BlockSpec((SEQ, w), lambda b, h, j: (b, h)),
            pl.BlockSpec((SEQ, w), lambda b, h, j: (b, vb_col0 + h)),
            pl.BlockSpec((2 * CK, CK), lambda b, h, j: (0, 0)),
        ],
        out_specs=pl.BlockSpec((Q_BLK, w), lambda b, h, j: (b * rb + j, h)),
        out_shape=jax.ShapeDtypeStruct((M_ALL, W_B), F32),
        compiler_params=_cparams(3),
        name="stick_breaking_prompt",
    )(qb, kvb_bf, kvb_bf, ustrict2)


def _sel_sample_kernel(pt_ref, qi_ref, wi_ref, *rest):
    page_refs = rest[:N_PAGES]
    knew_ref, dup_ref, utri_ref, am_ref, key_ref = rest[N_PAGES:]
    s = pl.program_id(1)
    rows8 = lax.broadcasted_iota(I32, (SUBLANES, LANES), 0)
    cols8 = lax.broadcasted_iota(I32, (SUBLANES, LANES), 1)
    qi = qi_ref[...]
    wcol = wi_ref[...]
    row0 = pl.multiple_of(s * SUBLANES, SUBLANES)
    for p in range(S_TILES):
        if p < N_PAGES:
            dots = _dot(qi, page_refs[p][...].astype(BF16))
            adm = rows8 < DEC_SEQ
        else:
            dots = _dot_nt(qi, knew_ref[...])
            adm = (rows8 < DEC_SEQ) & (cols8 <= rows8)
        r = jnp.maximum(dots, 0.0) * wcol
        sc = jnp.sum(r.reshape(SUBLANES, H_I, LANES), axis=1)
        key_ref[pl.ds(row0, SUBLANES), p * LANES:(p + 1) * LANES] = _sortable(
            jnp.where(adm, sc, -jnp.inf))

    @pl.when(s == SEQ_GROUP - 1)
    def _():
        nrow = SEQ_GROUP * SUBLANES
        rowsg = lax.broadcasted_iota(I32, (nrow, LANES), 0) % SUBLANES
        colsg = lax.broadcasted_iota(I32, (nrow, LANES), 1)
        adm_past = rowsg < DEC_SEQ
        adm_new = adm_past & (colsg <= rowsg)

        def adm_of(p):
            return adm_past if p < N_PAGES else adm_new

        def count_ge(cands):
            cbs = [jnp.broadcast_to(cand, (nrow, LANES)) for cand in cands]
            accs = [jnp.zeros((nrow, LANES), F32) for _ in cands]
            for p in range(S_TILES):
                key = key_ref[:, p * LANES:(p + 1) * LANES]
                accs = [acc + jnp.where(key >= cb, 1.0, 0.0) for acc, cb in zip(accs, cbs)]
            return [jnp.sum(acc, axis=1, keepdims=True) for acc in accs]

        pad_rows = (lax.broadcasted_iota(I32, (nrow, 1), 0) % SUBLANES) >= DEC_SEQ
        thr = _topk_threshold(count_ge, float(S_TILES * LANES), pad_rows)
        tb = jnp.broadcast_to(thr, (nrow, LANES))

        def write(p, sel):
            dup = _dot(sel.astype(BF16), dup_ref[...])
            am_ref[:, p * CK:(p + 1) * CK] = jnp.where(dup > 0.5, 0.0, NEG)

        n_sel = jnp.zeros((nrow, LANES), F32)
        for p in range(S_TILES):
            sel = jnp.where(adm_of(p),
                            jnp.where(key_ref[:, p * LANES:(p + 1) * LANES] >= tb, 1.0, 0.0), 0.0)
            n_sel = n_sel + sel
            write(p, sel)
        n_sel = jnp.sum(n_sel, axis=1, keepdims=True)

        @pl.when(jnp.max(n_sel) > TOPK)
        def _():
            n_gt = jnp.zeros((nrow, LANES), F32)
            for p in range(S_TILES):
                n_gt = n_gt + jnp.where(
                    adm_of(p),
                    jnp.where(key_ref[:, p * LANES:(p + 1) * LANES] > tb, 1.0, 0.0), 0.0)
            keep = TOPK - jnp.sum(n_gt, axis=1, keepdims=True)
            seen = jnp.zeros((nrow, 1), F32)
            for p in range(S_TILES):
                key = key_ref[:, p * LANES:(p + 1) * LANES]
                eq = jnp.where(adm_of(p), jnp.where(key == tb, 1.0, 0.0), 0.0)
                rank = seen + _dot(eq.astype(BF16), utri_ref[...])
                sel = jnp.where(adm_of(p), jnp.where(key > tb, 1.0, 0.0), 0.0)
                sel = jnp.where(eq > 0.0, jnp.where(rank <= keep, 1.0, 0.0), sel)
                write(p, sel)
                seen = seen + jnp.sum(eq, axis=1, keepdims=True)


def _sel_sample(page_table, layer, qi_s, wi_s, cache_kidx, kidx_new, dup, utri):
    def page_map(p):
        return lambda g, s, pt: (pt[g * SEQ_GROUP + s, p], layer, 0, 0)

    seq_map = lambda g, s, pt: (g * SEQ_GROUP + s, 0, 0)
    grid_spec = pltpu.PrefetchScalarGridSpec(
        num_scalar_prefetch=1,
        grid=(N_SEQ_GROUPS, SEQ_GROUP),
        in_specs=[pl.BlockSpec((None, LANES, D_IDX), seq_map),
                  pl.BlockSpec((None, LANES, 1), seq_map)]
        + [pl.BlockSpec((None, None, D_IDX, PAGE_SIZE), page_map(p)) for p in range(N_PAGES)]
        + [pl.BlockSpec((None, LANES, D_IDX), seq_map),
           pl.BlockSpec((LANES, CK), lambda g, s, pt: (0, 0)),
           pl.BlockSpec((LANES, LANES), lambda g, s, pt: (0, 0))],
        out_specs=pl.BlockSpec((None, SEQ_GROUP * SUBLANES, S_TILES * CK),
                               lambda g, s, pt: (g, 0, 0)),
        scratch_shapes=[pltpu.VMEM((SEQ_GROUP * SUBLANES, S_TILES * LANES), I32)],
    )
    return pl.pallas_call(
        _sel_sample_kernel,
        grid_spec=grid_spec,
        out_shape=jax.ShapeDtypeStruct((N_SEQ_GROUPS, SEQ_GROUP * SUBLANES, S_TILES * CK), F32),
        compiler_params=_cparams(2),
        name="indexer_select_sample",
    )(page_table, qi_s, wi_s, *([cache_kidx] * N_PAGES), kidx_new, dup, utri)


def _decode_kernel(pt_ref, qb_ref, qa_ref, kbn_ref, vbn_ref, kan_ref, van_ref, am_ref, amn_ref,
                   strip_ref, far_ref, uj2_ref, *rest):
    P = PAGES_PER_STEP
    kb_refs, vb_refs = rest[0:P], rest[P:2 * P]
    ka_refs, va_refs = rest[2 * P:3 * P], rest[3 * P:4 * P]
    ob_ref, oa_ref, accb_ref, carry_ref, m_ref, l_ref, acca_ref = rest[4 * P:]
    ci = pl.program_id(1)
    nrow = DEC_SEQ * H_B

    rows = lax.broadcasted_iota(I32, (nrow, CK), 0)
    lan = lax.broadcasted_iota(I32, (nrow, CK), 1)
    head_b = (lan % H_B) == (rows % H_B)
    group_a = (lan % KV_A) == ((rows % H_A) // G_A)
    qb = qb_ref[...]
    qa = qa_ref[...]

    def am_rows(am8):
        return jnp.concatenate(
            [jnp.broadcast_to(am8[q:q + 1, :], (H_A, am8.shape[1])) for q in range(DEC_SEQ)], axis=0)

    def dsa_update(lg, pv):
        m = m_ref[...]
        m_new = jnp.maximum(m, jnp.max(lg, axis=1, keepdims=True))
        alpha = jnp.exp(m - m_new)
        p = jnp.exp(lg - m_new)
        l_ref[...] = alpha * l_ref[...] + jnp.sum(p, axis=1, keepdims=True)
        acca_ref[...] = alpha * acca_ref[...] + pv(p.astype(BF16))
        m_ref[...] = m_new

    @pl.when(ci == 0)
    def _():
        m_ref[...] = jnp.full((nrow, 1), -jnp.inf, F32)
        l_ref[...] = jnp.zeros((nrow, 1), F32)
        acca_ref[...] = jnp.zeros((nrow, HEAD_DIM), F32)
        rows1 = lax.broadcasted_iota(I32, (nrow, LANES), 0)
        lan1 = lax.broadcasted_iota(I32, (nrow, LANES), 1)
        head_b1 = (lan1 % H_B) == (rows1 % H_B)
        group_a1 = (lan1 % KV_A) == ((rows1 % H_A) // G_A)
        vis = head_b1 & ((lan1 // H_B) < (rows1 // H_B))
        z = _dot_nt(qb, kbn_ref[...]) * ATT_SCALE
        sp = _softplus(z)
        lm = jnp.where(vis, -sp, 0.0)
        hl = _hi_lo(lm)
        u1 = jnp.concatenate([uj2_ref[:LANES, :LANES], uj2_ref[:LANES, :LANES]], axis=0)
        j1 = jnp.concatenate([uj2_ref[:LANES, CK:CK + LANES], uj2_ref[:LANES, CK:CK + LANES]], axis=0)
        a = jnp.where(vis, jnp.exp(z - sp + _dot(hl, u1)), 0.0)
        tot = _dot(hl, j1)
        accb_ref[...] = _dot(a.astype(BF16), vbn_ref[...])
        carry_ref[...] = jnp.concatenate([tot, tot], axis=1)
        lg = _dot_nt(qa, kan_ref[...]) * ATT_SCALE + strip_ref[:, CK:CK + LANES] \
            + am_rows(amn_ref[...])[:, :LANES]
        dsa_update(jnp.where(group_a1, lg, NEG), lambda p: _dot(p, van_ref[...]))

    nsub = PAGE_SIZE * H_B // CK
    blocks = []
    for i in range(P):
        z = _dot_nt(qb, kb_refs[i][...].astype(BF16)) * ATT_SCALE
        for sb in reversed(range(nsub)):
            blocks.append(z[:, sb * CK:(sb + 1) * CK])
    sps = [_softplus(zb) for zb in blocks]
    lms = [jnp.where(head_b, -sp, 0.0) for sp in sps]
    r = _dot(jnp.concatenate([_hi_lo(lm) for lm in lms], axis=0), uj2_ref[...])
    carry = carry_ref[...]
    a_blocks = []
    for k, zb in enumerate(blocks):
        within = r[k * nrow:(k + 1) * nrow, :CK]
        a_blocks.append(jnp.where(head_b, jnp.exp(zb - sps[k] + within + carry), 0.0).astype(BF16))
        carry = carry + r[k * nrow:(k + 1) * nrow, CK:]
    carry_ref[...] = carry
    accb = accb_ref[...]
    for i in range(P):
        a_page = jnp.concatenate(a_blocks[i * nsub:(i + 1) * nsub][::-1], axis=1)
        accb = accb + _dot(a_page, vb_refs[i][...].astype(BF16))
    accb_ref[...] = accb

    far = jnp.broadcast_to(far_ref[...][:, :1], (nrow, CK))
    am_all = am_rows(am_ref[...])
    lgs = []
    for i in range(P):
        lane0 = (P - 1 - i) * CK
        bias = jnp.where(ci == 0, strip_ref[:, :CK], far) if i == 0 else far
        lg = _dot_nt(qa, ka_refs[i][...].astype(BF16)) * ATT_SCALE + bias \
            + am_all[:, lane0:lane0 + CK]
        lgs.append(jnp.where(group_a, lg, NEG))

    def pv_pages(p):
        out = jnp.zeros((nrow, HEAD_DIM), F32)
        for i in range(P):
            out = out + _dot(p[:, i * CK:(i + 1) * CK], va_refs[i][...].astype(BF16))
        return out

    dsa_update(jnp.concatenate(lgs, axis=1), pv_pages)

    @pl.when(ci == N_PAGE_STEPS - 1)
    def _():
        ob_ref[...] = accb_ref[...]
        oa_ref[...] = acca_ref[...] / l_ref[...]


def _decode(page_table, layer, qb_s, qa_s, kb_new, vb_new, ka_new, va_new, am_s, strip, far,
            uj2, ckb, cvb, cka, cva):
    P = PAGES_PER_STEP

    def page_map(i, ndim):
        def f(s, ci, pt):
            return (pt[s, N_PAGES - 1 - (P * ci + i)], layer) + (0,) * (ndim - 2)
        return f

    seq_map = lambda s, ci, pt: (s, 0, 0)
    const2 = lambda s, ci, pt: (0, 0)
    nrow = DEC_SEQ * H_B
    rows_b = PAGE_SIZE * H_B
    rows_a = PAGE_SIZE * KV_A
    in_specs = [
        pl.BlockSpec((None, nrow, HEAD_DIM), seq_map),
        pl.BlockSpec((None, nrow, HEAD_DIM), seq_map),
        pl.BlockSpec((None, LANES, HEAD_DIM), seq_map),
        pl.BlockSpec((None, LANES, HEAD_DIM), seq_map),
        pl.BlockSpec((None, LANES, HEAD_DIM), seq_map),
        pl.BlockSpec((None, LANES, HEAD_DIM), seq_map),
        pl.BlockSpec((None, SUBLANES, P * CK), lambda s, ci, pt: (s, 0, N_PAGE_STEPS - 1 - ci)),
        pl.BlockSpec((None, SUBLANES, CK), lambda s, ci, pt: (s, 0, N_PAGES)),
        pl.BlockSpec((nrow, 2 * CK), const2),
        pl.BlockSpec((nrow, LANES), const2),
        pl.BlockSpec((2 * CK, 2 * CK), const2),
    ]
    in_specs += [pl.BlockSpec((None, None, rows_b, HEAD_DIM), page_map(i, 4)) for i in range(P)] * 2
    in_specs += [pl.BlockSpec((None, None, rows_a, HEAD_DIM), page_map(i, 4)) for i in range(P)] * 2
    grid_spec = pltpu.PrefetchScalarGridSpec(
        num_scalar_prefetch=1,
        grid=(DEC_BATCH, N_PAGE_STEPS),
        in_specs=in_specs,
        out_specs=[pl.BlockSpec((None, nrow, HEAD_DIM), seq_map)] * 2,
        scratch_shapes=[pltpu.VMEM((nrow, HEAD_DIM), F32), pltpu.VMEM((nrow, CK), F32),
                        pltpu.VMEM((nrow, 1), F32), pltpu.VMEM((nrow, 1), F32),
                        pltpu.VMEM((nrow, HEAD_DIM), F32)],
    )
    return pl.pallas_call(
        _decode_kernel,
        grid_spec=grid_spec,
        out_shape=[jax.ShapeDtypeStruct((DEC_BATCH, nrow, HEAD_DIM), F32)] * 2,
        compiler_params=_cparams(2),
        name="decode_attention_sample",
    )(page_table, qb_s, qa_s, kb_new, vb_new, ka_new, va_new, am_s, am_s, strip, far, uj2,
      *([ckb] * P), *([cvb] * P), *([cka] * P), *([cva] * P))


def _merge_kernel(oa_ref, ga_ref, ob_ref, gb_ref, ma_ref, mb_ref, wpa_ref, wpb_ref, u_ref):
    ga = ga_ref[...]
    gb = gb_ref[...]
    ya = _dot((oa_ref[...] * (ga * jax.nn.sigmoid(ga))).astype(BF16), wpa_ref[...])
    yb = _dot((ob_ref[...] * (gb * jax.nn.sigmoid(gb))).astype(BF16), wpb_ref[...])
    u = jax.nn.sigmoid(ma_ref[...]) * ya + jax.nn.sigmoid(mb_ref[...]) * yb
    u_ref[...] = u.astype(BF16)


def _merge(o_a, o_b, ga, gb, mab, w_pa, w_pb, tm=256):
    return pl.pallas_call(
        _merge_kernel,
        grid=(M_ALL // tm,),
        in_specs=[pl.BlockSpec((tm, W_A), lambda i: (i, 0)),
                  pl.BlockSpec((tm, W_A), lambda i: (i, 0)),
                  pl.BlockSpec((tm, W_B), lambda i: (i, 0)),
                  pl.BlockSpec((tm, W_B), lambda i: (i, 0)),
                  pl.BlockSpec((tm, D_MODEL), lambda i: (i, 0)),
                  pl.BlockSpec((tm, D_MODEL), lambda i: (i, 1)),
                  pl.BlockSpec((W_A, D_MODEL), lambda i: (0, 0)),
                  pl.BlockSpec((W_B, D_MODEL), lambda i: (0, 0))],
        out_specs=pl.BlockSpec((tm, D_MODEL), lambda i: (i, 0)),
        out_shape=jax.ShapeDtypeStruct((M_ALL, D_MODEL), BF16),
        compiler_params=_cparams(1),
        name="gated_merge",
    )(o_a, ga, o_b, gb, mab, mab, w_pa, w_pb)


def _outproj_kernel(u_ref, x_ref, w_ref, g_ref, b_ref, o_ref):
    mix = _dot(u_ref[...], w_ref[...])
    o_ref[...] = _layer_norm_rows(ALPHA * x_ref[...] + mix, g_ref[...], b_ref[...])


def _outproj(u, x, w_out, g, b, tm=512):
    return pl.pallas_call(
        _outproj_kernel,
        grid=(M_ALL // tm,),
        in_specs=[pl.BlockSpec((tm, D_MODEL), lambda i: (i, 0)),
                  pl.BlockSpec((tm, D_MODEL), lambda i: (i, 0)),
                  pl.BlockSpec((D_MODEL, D_MODEL), lambda i: (0, 0)),
                  pl.BlockSpec((1, D_MODEL), lambda i: (0, 0)),
                  pl.BlockSpec((1, D_MODEL), lambda i: (0, 0))],
        out_specs=pl.BlockSpec((tm, D_MODEL), lambda i: (i, 0)),
        out_shape=jax.ShapeDtypeStruct((M_ALL, D_MODEL), F32),
        compiler_params=_cparams(1),
        name="out_proj_norm",
    )(u, x, w_out, g, b)


def _ple_kernel(x_ref, pp_ref, ps_ref, wg_ref, we_ref, g_ref, b_ref, o_ref, ob_ref):
    x = x_ref[...]
    is_prompt = pl.program_id(0) < M_PROMPT // x_ref.shape[0]
    p = jnp.where(is_prompt, pp_ref[...], ps_ref[...])
    gate = jax.nn.sigmoid(_dot(x.astype(BF16), wg_ref[...]))
    ple = gate * _dot(p.astype(BF16), we_ref[...])
    y = _layer_norm_rows(ALPHA * x + ple, g_ref[...], b_ref[...])
    o_ref[...] = y
    ob_ref[...] = y.astype(BF16)


def _ple(x, p_prompt, p_sample, layer, w_pg, w_pe, g, b, tm=256):
    npb = M_PROMPT // tm
    return pl.pallas_call(
        _ple_kernel,
        grid=(M_ALL // tm,),
        in_specs=[pl.BlockSpec((tm, D_MODEL), lambda i: (i, 0)),
                  pl.BlockSpec((None, tm, PLE_DIM), lambda i: (layer, jnp.minimum(i, npb - 1), 0)),
                  pl.BlockSpec((None, tm, PLE_DIM), lambda i: (layer, jnp.maximum(i - npb, 0), 0)),
                  pl.BlockSpec((D_MODEL, D_MODEL), lambda i: (0, 0)),
                  pl.BlockSpec((PLE_DIM, D_MODEL), lambda i: (0, 0)),
                  pl.BlockSpec((1, D_MODEL), lambda i: (0, 0)),
                  pl.BlockSpec((1, D_MODEL), lambda i: (0, 0))],
        out_specs=[pl.BlockSpec((tm, D_MODEL), lambda i: (i, 0))] * 2,
        out_shape=[jax.ShapeDtypeStruct((M_ALL, D_MODEL), F32),
                   jax.ShapeDtypeStruct((M_ALL, D_MODEL), BF16)],
        compiler_params=_cparams(1),
        name="ple_update_norm",
    )(x, p_prompt, p_sample, w_pg, w_pe, g, b)


def _tri_constants():
    a = jnp.arange(CK, dtype=I32)
    ustrict = (a[:, None] > a[None, :]).astype(BF16)
    ustrict2 = jnp.concatenate([ustrict, ustrict], axis=0)
    utri = (a[:, None] <= a[None, :]).astype(BF16)
    same_head = (a[:, None] % H_B) == (a[None, :] % H_B)
    u_b = (same_head & ((a[:, None] // H_B) > (a[None, :] // H_B))).astype(BF16)
    j_b = same_head.astype(BF16)
    uj = jnp.concatenate([u_b, j_b], axis=1)
    uj2 = jnp.concatenate([uj, uj], axis=0)
    s = jnp.arange(LANES, dtype=I32)
    dup = (s[:, None] == (a[None, :] // KV_A)).astype(BF16)
    return ustrict2, utri, uj2, dup


def kernel(x_prompt, x_sample, cache_k_a, cache_v_a, cache_kidx, cache_k_b, cache_v_b, page_table,
           p_prompt, p_sample, w_in, ln_kidx_g, ln_kidx_b, rel_bias, w_pa, w_pb, w_out,
           ln1_g, ln1_b, w_pe, w_pg, ln2_g, ln2_b):
    n_pool = cache_k_a.shape[0]
    ckb = cache_k_b.reshape(n_pool, DEPTH, PAGE_SIZE * H_B, HEAD_DIM)
    cvb = cache_v_b.reshape(n_pool, DEPTH, PAGE_SIZE * H_B, HEAD_DIM)
    cka = cache_k_a.reshape(n_pool, DEPTH, PAGE_SIZE * KV_A, HEAD_DIM)
    cva = cache_v_a.reshape(n_pool, DEPTH, PAGE_SIZE * KV_A, HEAD_DIM)
    ckidx_t = jnp.swapaxes(cache_kidx, 2, 3)

    ustrict2, utri, uj2, dup = _tri_constants()
    bias_tiles, strip, far = _bias_tables(rel_bias)

    kv_w = KV_A * HEAD_DIM
    o_qa, o_kva, o_ga = 0, W_A, W_A + 2 * kv_w
    o_qi = o_ga + W_A
    o_ki = o_qi + H_I * D_IDX
    o_hi = o_ki + D_IDX + H_I
    h_qb, h_kvb, h_gb, h_mab = 0, W_B, 3 * W_B, 4 * W_B
    w_t = jnp.swapaxes(w_in, 1, 2)
    w_lo = w_t[:, :o_ki + LANES].astype(BF16)
    w_hi = w_t[:, o_hi:].astype(BF16)

    x = jnp.concatenate([x_prompt.reshape(M_PROMPT, D_MODEL), x_sample.reshape(M_SAMPLE, D_MODEL)], 0)
    pp_all = p_prompt.reshape(DEPTH, M_PROMPT, PLE_DIM)
    ps_all = p_sample.reshape(DEPTH, M_SAMPLE, PLE_DIM)

    xb = x.astype(BF16)
    rows_p = {n: [] for n in ("k_a", "v_a", "kidx", "k_b", "v_b")}
    rows_s = {n: [] for n in ("k_a", "v_a", "kidx", "k_b", "v_b")}

    for i in range(DEPTH):
        g2 = jnp.concatenate([ln_kidx_g[i], ln_kidx_g[i]])[None, :]
        b2 = jnp.concatenate([ln_kidx_b[i], ln_kidx_b[i]])[None, :]

        (qa,) = _proj(xb, w_lo, i, o_qa, W_A, [BF16])
        kva_f, kva_bf = _proj(xb, w_lo, i, o_kva, 2 * kv_w, [F32, BF16])
        (ga,) = _proj(xb, w_lo, i, o_ga, W_A, [F32])
        (qi,) = _proj(xb, w_lo, i, o_qi, H_I * D_IDX, [BF16])
        kidx2, ki2_bf, wi_all = _kiw(xb, w_lo, i, o_ki, g2, b2)
        (qb,) = _proj(xb, w_hi, i, h_qb, W_B, [BF16])
        kvb_f, kvb_bf = _proj(xb, w_hi, i, h_kvb, 2 * W_B, [F32, BF16])
        (gb,) = _proj(xb, w_hi, i, h_gb, W_B, [F32])
        (mab,) = _proj(xb, w_hi, i, h_mab, 2 * D_MODEL, [F32])

        rows_p["k_a"].append(kva_f[:M_PROMPT, :kv_w].reshape(BATCH, SEQ, KV_A, HEAD_DIM))
        rows_p["v_a"].append(kva_f[:M_PROMPT, kv_w:].reshape(BATCH, SEQ, KV_A, HEAD_DIM))
        rows_p["k_b"].append(kvb_f[:M_PROMPT, :W_B].reshape(BATCH, SEQ, H_B, HEAD_DIM))
        rows_p["v_b"].append(kvb_f[:M_PROMPT, W_B:].reshape(BATCH, SEQ, H_B, HEAD_DIM))
        rows_p["kidx"].append(kidx2[:M_PROMPT, :D_IDX].reshape(BATCH, SEQ, D_IDX))
        rows_s["k_a"].append(kva_f[M_PROMPT:, :kv_w].reshape(DEC_BATCH, DEC_SEQ, KV_A, HEAD_DIM))
        rows_s["v_a"].append(kva_f[M_PROMPT:, kv_w:].reshape(DEC_BATCH, DEC_SEQ, KV_A, HEAD_DIM))
        rows_s["k_b"].append(kvb_f[M_PROMPT:, :W_B].reshape(DEC_BATCH, DEC_SEQ, H_B, HEAD_DIM))
        rows_s["v_b"].append(kvb_f[M_PROMPT:, W_B:].reshape(DEC_BATCH, DEC_SEQ, H_B, HEAD_DIM))
        rows_s["kidx"].append(kidx2[M_PROMPT:, :D_IDX].reshape(DEC_BATCH, DEC_SEQ, D_IDX))

        oa_p = _dsa_prompt(qa, qi, wi_all, ki2_bf, kva_bf, bias_tiles, utri)
        ob_p = _sb_prompt(qb, kvb_bf, ustrict2)

        kvs_bf = jnp.concatenate([kva_bf[M_PROMPT:], kvb_bf[M_PROMPT:]], axis=1)
        qa_s = qa[M_PROMPT:].reshape(DEC_BATCH, DEC_SEQ * H_A, HEAD_DIM)
        qb_s = qb[M_PROMPT:].reshape(DEC_BATCH, DEC_SEQ * H_B, HEAD_DIM)
        qi_s = qi[M_PROMPT:].reshape(DEC_BATCH, DEC_SEQ * H_I, D_IDX)
        qi_s = jnp.pad(qi_s, ((0, 0), (0, LANES - DEC_SEQ * H_I), (0, 0)))
        wi_s = wi_all[M_PROMPT:, D_IDX:D_IDX + H_I].reshape(DEC_BATCH, DEC_SEQ * H_I, 1)
        wi_s = jnp.pad(wi_s, ((0, 0), (0, LANES - DEC_SEQ * H_I), (0, 0)))
        kidx_new = ki2_bf[M_PROMPT:, :D_IDX].reshape(DEC_BATCH, DEC_SEQ, D_IDX)
        kidx_new = jnp.pad(kidx_new, ((0, 0), (0, LANES - DEC_SEQ), (0, 0)))

        def new_tile(a, heads, slots):
            a = a.reshape(DEC_BATCH, DEC_SEQ, heads, HEAD_DIM)
            a = jnp.pad(a, ((0, 0), (0, slots - DEC_SEQ), (0, 0), (0, 0)))
            return a.reshape(DEC_BATCH, slots * heads, HEAD_DIM)

        ka_new = new_tile(kvs_bf[:, 0:256], KV_A, NEW_A)
        va_new = new_tile(kvs_bf[:, 256:512], KV_A, NEW_A)
        kb_new = new_tile(kvs_bf[:, 512:512 + W_B], H_B, NEW_B)
        vb_new = new_tile(kvs_bf[:, 512 + W_B:], H_B, NEW_B)

        am = _sel_sample(page_table, i, qi_s, wi_s, ckidx_t, kidx_new, dup, utri[:LANES, :LANES])
        am_s = am.reshape(DEC_BATCH, SUBLANES, S_TILES * CK)
        ob_s, oa_s = _decode(page_table, i, qb_s, qa_s, kb_new, vb_new, ka_new, va_new, am_s,
                             strip, far, uj2, ckb, cvb, cka, cva)

        o_a = lax.dynamic_update_slice(oa_p, oa_s.reshape(M_SAMPLE, W_A), (M_PROMPT, 0))
        o_b = lax.dynamic_update_slice(ob_p, ob_s.reshape(M_SAMPLE, W_B), (M_PROMPT, 0))

        u = _merge(o_a, o_b, ga, gb, mab, w_pa[i].astype(BF16), w_pb[i].astype(BF16))
        x = _outproj(u, x, w_out[i].astype(BF16), ln1_g[i][None, :], ln1_b[i][None, :])
        x, xb = _ple(x, pp_all, ps_all, i, w_pg[i].astype(BF16), w_pe[i].astype(BF16),
                     ln2_g[i][None, :], ln2_b[i][None, :])

    st = lambda rows: jnp.stack(rows, axis=1)
    return (x[:M_PROMPT].reshape(BATCH, SEQ, D_MODEL), x[M_PROMPT:].reshape(DEC_BATCH, DEC_SEQ, D_MODEL),
            st(rows_p["k_a"]), st(rows_p["v_a"]), st(rows_p["kidx"]), st(rows_p["k_b"]), st(rows_p["v_b"]),
            st(rows_s["k_a"]), st(rows_s["v_a"]), st(rows_s["kidx"]), st(rows_s["k_b"]), st(rows_s["v_b"]))
```
